```python
import jax, jax.numpy as jnp
from jax import lax
import numpy as np

D_MODEL = 1024
BATCH = 16
SEQ = 4096
DEPTH = 2

FOX_HEADS = 8
FOX_HEAD_DIM = 64
FOX_WIDTH = FOX_HEADS * FOX_HEAD_DIM
GDN_HEADS = 4
GDN_HEAD_DIM = 128
GDN_WIDTH = GDN_HEADS * GDN_HEAD_DIM
MIX_WIDTH = FOX_WIDTH + GDN_WIDTH
CONV_WIDTH = 4
CHUNK = 64
Q_BLOCK = 128
D_FF = 2816
EPS = 1e-6
FORGET_BIAS_INIT = 3.0
SPLIT_SIZES = (FOX_WIDTH, FOX_WIDTH, FOX_WIDTH, FOX_HEADS,
               GDN_WIDTH, GDN_WIDTH, GDN_WIDTH, GDN_HEADS, GDN_HEADS, GDN_WIDTH)
N_IN = 3 * FOX_WIDTH + FOX_HEADS + 4 * GDN_WIDTH + 2 * GDN_HEADS

kernel_name = "macaron_fox_gdn_hybrid"


def rms_norm(x, w):
    xf = x.astype(jnp.float32)
    y = xf * lax.rsqrt(jnp.mean(xf * xf, axis=-1, keepdims=True) + EPS)
    return (y * w.astype(jnp.float32)).astype(x.dtype)


def l2_norm(x):
    xf = x.astype(jnp.float32)
    return xf * lax.rsqrt(jnp.sum(xf * xf, axis=-1, keepdims=True) + EPS)


def swiglu_ffn(x, w_in, w_out):
    gate, up = jnp.split(x @ w_in, 2, axis=-1)
    return (jax.nn.silu(gate) * up) @ w_out


def causal_depthwise_conv(x, w):
    c = x.shape[-1]
    return lax.conv_general_dilated(
        x, w[:, None, :].astype(x.dtype), window_strides=(1,), padding=[(CONV_WIDTH - 1, 0)],
        dimension_numbers=("NWC", "WIO", "NWC"), feature_group_count=c)


def forgetting_attention(q, k, v, f_logit):
    seq = q.shape[2]
    scale = FOX_HEAD_DIM ** -0.5
    cum = jnp.cumsum(jax.nn.log_sigmoid(f_logit.astype(jnp.float32)), axis=-1)
    outs = []
    for blk in range(seq // Q_BLOCK):
        start, end = blk * Q_BLOCK, (blk + 1) * Q_BLOCK
        qb = q[:, :, start:end]
        kb = k[:, :, :end]
        vb = v[:, :, :end]
        s = jnp.einsum("bhqd,bhkd->bhqk", qb, kb).astype(jnp.float32) * scale
        s = s + cum[:, :, start:end, None] - cum[:, :, None, :end]
        causal = (start + jnp.arange(Q_BLOCK))[:, None] >= jnp.arange(end)[None, :]
        s = jnp.where(causal, s, -jnp.inf)
        p = jax.nn.softmax(s, axis=-1)
        outs.append(jnp.einsum("bhqk,bhkd->bhqd", p.astype(v.dtype), vb))
    return jnp.concatenate(outs, axis=2)


def gated_delta_rule_chunked(q, k, v, g, beta):
    out_dtype = v.dtype
    b, h, seq, dk = q.shape
    dv = v.shape[-1]
    n = seq // CHUNK
    q = q.astype(jnp.float32) * dk ** -0.5
    k = k.astype(jnp.float32)
    v = v.astype(jnp.float32)
    g = g.astype(jnp.float32).reshape(b, h, n, CHUNK)
    beta = beta.astype(jnp.float32).reshape(b, h, n, CHUNK)
    q = q.reshape(b, h, n, CHUNK, dk)
    k = k.reshape(b, h, n, CHUNK, dk)
    v = v.reshape(b, h, n, CHUNK, dv)

    g = jnp.cumsum(g, axis=-1)
    tri_incl = jnp.tril(jnp.ones((CHUNK, CHUNK), dtype=bool))
    tri_strict = jnp.tril(jnp.ones((CHUNK, CHUNK), dtype=bool), k=-1)
    decay = jnp.exp(jnp.where(tri_incl, g[..., :, None] - g[..., None, :], -jnp.inf))

    k_beta = k * beta[..., None]
    v_beta = v * beta[..., None]
    a_strict = jnp.where(tri_strict, jnp.einsum("bhnid,bhnjd->bhnij", k_beta, k) * decay, 0.0)
    eye = jnp.eye(CHUNK, dtype=jnp.float32)
    t_mat = lax.linalg.triangular_solve(eye + a_strict, jnp.broadcast_to(eye, a_strict.shape),
                                        left_side=True, lower=True, unit_diagonal=True)
    u = jnp.einsum("bhnij,bhnjd->bhnid", t_mat, v_beta)
    w = jnp.einsum("bhnij,bhnjd->bhnid", t_mat, k_beta * jnp.exp(g)[..., None])
    attn_intra = jnp.where(tri_incl, jnp.einsum("bhnid,bhnjd->bhnij", q, k) * decay, 0.0)
    g_last = g[..., -1]
    k_dec = k * jnp.exp(g_last[..., None] - g)[..., None]
    q_dec = q * jnp.exp(g)[..., None]

    def step(state, inp):
        qd, kd, uc, wc, ac, gl = inp
        v_new = uc - jnp.einsum("bhck,bhkv->bhcv", wc, state)
        o = jnp.einsum("bhck,bhkv->bhcv", qd, state) + jnp.einsum("bhij,bhjv->bhiv", ac, v_new)
        state = state * jnp.exp(gl)[..., None, None] + jnp.einsum("bhck,bhcv->bhkv", kd, v_new)
        return state, o

    xs = tuple(jnp.moveaxis(t, 2, 0) for t in (q_dec, k_dec, u, w, attn_intra, g_last))
    state0 = jnp.zeros((b, h, dk, dv), jnp.float32)
    _, o = lax.scan(step, state0, xs)
    o = jnp.moveaxis(o, 0, 2).reshape(b, h, seq, dv)
    return o.astype(out_dtype)


def hybrid_mixer(hn, w_in, fox_q_norm, fox_k_norm, fox_f_bias,
                 gdn_conv, gdn_a_log, gdn_dt_bias, gdn_out_norm, w_out):
    b, s, _ = hn.shape
    proj = hn @ w_in
    offsets = []
    acc = 0
    for size in SPLIT_SIZES[:-1]:
        acc += size
        offsets.append(acc)
    fq, fk, fv, ff, gq, gk, gv, ga, gb, gg = jnp.split(proj, offsets, axis=-1)

    def heads(t, n_h, d_h):
        return t.reshape(b, s, n_h, d_h).transpose(0, 2, 1, 3)

    fq = rms_norm(heads(fq, FOX_HEADS, FOX_HEAD_DIM), fox_q_norm)
    fk = rms_norm(heads(fk, FOX_HEADS, FOX_HEAD_DIM), fox_k_norm)
    fv = heads(fv, FOX_HEADS, FOX_HEAD_DIM)
    f_logit = (ff + fox_f_bias).transpose(0, 2, 1)
    y_fox = forgetting_attention(fq, fk, fv, f_logit)
    y_fox = y_fox.transpose(0, 2, 1, 3).reshape(b, s, FOX_WIDTH)

    qkv = jax.nn.silu(causal_depthwise_conv(jnp.concatenate([gq, gk, gv], axis=-1), gdn_conv))
    gq, gk, gv = jnp.split(qkv, 3, axis=-1)
    gq = l2_norm(heads(gq, GDN_HEADS, GDN_HEAD_DIM))
    gk = l2_norm(heads(gk, GDN_HEADS, GDN_HEAD_DIM))
    gv = heads(gv, GDN_HEADS, GDN_HEAD_DIM)
    beta = jax.nn.sigmoid(gb.astype(jnp.float32)).transpose(0, 2, 1)
    log_decay = (-jnp.exp(gdn_a_log.astype(jnp.float32))
                 * jax.nn.softplus(ga.astype(jnp.float32) + gdn_dt_bias.astype(jnp.float32))).transpose(0, 2, 1)
    y_gdn = gated_delta_rule_chunked(gq, gk, gv, log_decay, beta)
    y_gdn = rms_norm(y_gdn, gdn_out_norm) * jax.nn.silu(heads(gg, GDN_HEADS, GDN_HEAD_DIM))
    y_gdn = y_gdn.transpose(0, 2, 1, 3).reshape(b, s, GDN_WIDTH)

    return jnp.concatenate([y_fox, y_gdn], axis=-1) @ w_out


def setup_inputs(seed: int = 0) -> dict:
    key = jax.random.key(seed)
    ks = jax.random.split(key, 20)
    L, D = DEPTH, D_MODEL

    def normal(k, shape, scale):
        return jax.random.normal(k, shape, jnp.float32) * scale

    def gain(k, shape):
        return 1.0 + 0.1 * jax.random.normal(k, shape, jnp.float32)

    return {
        "x": normal(ks[0], (BATCH, SEQ, D), 1.0),
        "ffn1_norm": gain(ks[1], (L, D)),
        "ffn1_w_in": normal(ks[2], (L, D, 2 * D_FF), D ** -0.5),
        "ffn1_w_out": normal(ks[3], (L, D_FF, D), D_FF ** -0.5),
        "mix_norm": gain(ks[4], (L, D)),
        "w_in": normal(ks[5], (L, D, N_IN), D ** -0.5),
        "fox_q_norm": gain(ks[6], (L, FOX_HEAD_DIM)),
        "fox_k_norm": gain(ks[7], (L, FOX_HEAD_DIM)),
        "fox_f_bias": FORGET_BIAS_INIT + 0.1 * jax.random.normal(ks[8], (L, FOX_HEADS), jnp.float32),
        "gdn_conv": normal(ks[9], (L, CONV_WIDTH, 3 * GDN_WIDTH), CONV_WIDTH ** -0.5),
        "gdn_a_log": jnp.log(jax.random.uniform(ks[10], (L, GDN_HEADS), jnp.float32, 1.0, 16.0)),
        "gdn_dt_bias": jnp.log(jnp.expm1(jax.random.uniform(ks[11], (L, GDN_HEADS), jnp.float32, 0.001, 0.1))),
        "gdn_out_norm": gain(ks[12], (L, GDN_HEAD_DIM)),
        "w_out": normal(ks[13], (L, MIX_WIDTH, D), MIX_WIDTH ** -0.5),
        "ffn2_norm": gain(ks[14], (L, D)),
        "ffn2_w_in": normal(ks[15], (L, D, 2 * D_FF), D ** -0.5),
        "ffn2_w_out": normal(ks[16], (L, D_FF, D), D_FF ** -0.5),
    }


def reference(x, ffn1_norm, ffn1_w_in, ffn1_w_out, mix_norm, w_in, fox_q_norm, fox_k_norm,
              fox_f_bias, gdn_conv, gdn_a_log, gdn_dt_bias, gdn_out_norm, w_out,
              ffn2_norm, ffn2_w_in, ffn2_w_out):
    for l in range(DEPTH):
        x = x + 0.5 * swiglu_ffn(rms_norm(x, ffn1_norm[l]), ffn1_w_in[l], ffn1_w_out[l])
        x = x + hybrid_mixer(rms_norm(x, mix_norm[l]), w_in[l], fox_q_norm[l], fox_k_norm[l],
                             fox_f_bias[l], gdn_conv[l], gdn_a_log[l], gdn_dt_bias[l],
                             gdn_out_norm[l], w_out[l])
        x = x + 0.5 * swiglu_ffn(rms_norm(x, ffn2_norm[l]), ffn2_w_in[l], ffn2_w_out[l])
    return x
```

```python
import functools

import jax
import jax.numpy as jnp
from jax import lax
from jax.experimental import pallas as pl
from jax.experimental.pallas import tpu as pltpu

F32 = jnp.float32
BF16 = jnp.bfloat16

D_MODEL = 1024
FOX_HEADS = 8
FOX_HEAD_DIM = 64
FOX_WIDTH = FOX_HEADS * FOX_HEAD_DIM
GDN_HEADS = 4
GDN_HEAD_DIM = 128
GDN_WIDTH = GDN_HEADS * GDN_HEAD_DIM
CONV_WIDTH = 4
D_FF = 2816
EPS = 1e-6

LANES = 128
SUBLANES = 8
VMEM_LIMIT_BYTES = 56 * 1024 * 1024

FOX_PAD = FOX_HEADS * LANES
OFF_Q = 0
OFF_K = OFF_Q + FOX_PAD
OFF_V = OFF_K + FOX_PAD
OFF_G = OFF_V + FOX_PAD
OFF_GG = OFF_G + 3 * GDN_WIDTH
OFF_SMALL = OFF_GG + GDN_WIDTH
N_PROJ = OFF_SMALL + LANES

LANE_CUM = 0
LANE_GC = FOX_HEADS
LANE_BETA = LANE_GC + GDN_HEADS

BIAS_LANE = FOX_HEAD_DIM

GDN_CHUNK = 128
NEUMANN_STEPS = 6

FFN_TILE_M = 512
FFN_TILE_F = 256
PROJ_TILE_M = 512
ATTN_TILE = 512


def _dot(a, b):
    return jnp.dot(a, b, preferred_element_type=F32)


def _dot_nt(a, b):
    return lax.dot_general(a, b, (((1,), (1,)), ((), ())), preferred_element_type=F32)


def _split3(v):
    hi = v.astype(BF16)
    r1 = v - hi.astype(F32)
    mid = r1.astype(BF16)
    lo = (r1 - mid.astype(F32)).astype(BF16)
    return hi, mid, lo


def _rms_norm(x, gain_row):
    ms = jnp.mean(x * x, axis=-1, keepdims=True)
    return x * lax.rsqrt(ms + EPS) * gain_row


def _silu(x):
    return x / (1.0 + jnp.exp(-x))


def _const_spec(shape):
    nd = len(shape)
    return pl.BlockSpec(shape, lambda *_: (0,) * nd, pipeline_mode=pl.Buffered(1))


def _ffn_body(*refs, n_chunks, tile_f, has_mix):
    if has_mix:
        x_ref, yf_ref, yg_ref, wof_ref, wog_ref, g_ref, win_ref, wout_ref, o_ref, acc_ref = refs
    else:
        x_ref, g_ref, win_ref, wout_ref, o_ref, acc_ref = refs
    x = x_ref[...]
    if has_mix:
        x = x + _dot(yf_ref[...], wof_ref[...]) + _dot(yg_ref[...], wog_ref[...])
    xn = _rms_norm(x, g_ref[...]).astype(BF16)
    acc_ref[...] = jnp.zeros_like(acc_ref)

    def chunk(c, carry):
        h = _dot(xn, win_ref[c])
        gate = h[:, :tile_f]
        up = h[:, tile_f:]
        act = (_silu(gate) * up).astype(BF16)
        acc_ref[...] += _dot(act, wout_ref[c])
        return carry

    lax.fori_loop(0, n_chunks, chunk, 0)
    o_ref[...] = x + 0.5 * acc_ref[...]


def _ffn_call(x2d, gain, w_in, w_out, mix=None):
    m, d = x2d.shape
    tm = min(FFN_TILE_M, m)
    tf = FFN_TILE_F
    nc = D_FF // tf
    w_in_r = w_in.reshape(d, 2, nc, tf).transpose(2, 0, 1, 3).reshape(nc, d, 2 * tf).astype(BF16)
    w_out_r = w_out.reshape(nc, tf, d).astype(BF16)
    row = pl.BlockSpec((tm, d), lambda i: (i, 0))
    in_specs = [row]
    args = [x2d]
    if mix is not None:
        y_fox, y_gdn, wo_f, wo_g = mix
        in_specs += [pl.BlockSpec((tm, FOX_PAD), lambda i: (i, 0)),
                     pl.BlockSpec((tm, GDN_WIDTH), lambda i: (i, 0)),
                     _const_spec(wo_f.shape), _const_spec(wo_g.shape)]
        args += [y_fox, y_gdn, wo_f, wo_g]
    in_specs += [_const_spec((1, d)), _const_spec(w_in_r.shape), _const_spec(w_out_r.shape)]
    args += [gain.reshape(1, d), w_in_r, w_out_r]
    return pl.pallas_call(
        functools.partial(_ffn_body, n_chunks=nc, tile_f=tf, has_mix=mix is not None),
        grid=(m // tm,),
        in_specs=in_specs,
        out_specs=row,
        out_shape=jax.ShapeDtypeStruct((m, d), F32),
        scratch_shapes=[pltpu.VMEM((tm, d), F32)],
        compiler_params=pltpu.CompilerParams(dimension_semantics=("arbitrary",),
                                             vmem_limit_bytes=VMEM_LIMIT_BYTES),
        name="ffn_mix" if mix is not None else "ffn",
    )(*args)


def _proj_body(x_ref, g_ref, w_ref, cw_ref, qg_ref, kg_ref, brow_ref, alog_ref, tri_ref, btri_ref,
               pmat_ref, oneq_ref, onek_ref,
               qp_ref, kp_ref, vp_ref, gq_ref, gk_ref, gv_ref, sgg_ref, small_ref, smallt_ref,
               cum_carry, conv_buf, *, tm):
    s = pl.program_id(1)

    @pl.when(s == 0)
    def _():
        cum_carry[...] = jnp.zeros_like(cum_carry)
        conv_buf[0:SUBLANES, :] = jnp.zeros((SUBLANES, 3 * GDN_WIDTH), F32)

    xn = _rms_norm(x_ref[0], g_ref[...]).astype(BF16)
    lane = lax.broadcasted_iota(jnp.int32, (tm, LANES), 1)

    z = _dot(xn, w_ref[:, OFF_SMALL:OFF_SMALL + LANES]) + brow_ref[...]
    e = jnp.log1p(jnp.exp(-jnp.abs(z)))
    log_sig = jnp.minimum(z, 0.0) - e
    softplus = jnp.maximum(z, 0.0) + e
    sigmoid = 1.0 / (1.0 + jnp.exp(-z))
    log_decay = -jnp.exp(alog_ref[...]) * softplus
    v = jnp.where(lane < LANE_GC, log_sig, jnp.where(lane < LANE_BETA, log_decay, 0.0))
    v3 = jnp.concatenate(_split3(v), axis=1)
    c_seq = _dot(tri_ref[...], v3)
    c_blk = _dot(btri_ref[...], v3)
    cum = (c_seq[:, :LANES] + c_seq[:, LANES:2 * LANES] + c_seq[:, 2 * LANES:]) + cum_carry[...]
    gc = c_blk[:, :LANES] + c_blk[:, LANES:2 * LANES] + c_blk[:, 2 * LANES:]
    cum_carry[...] = cum[tm - 1:tm, :]
    small = jnp.where(lane < LANE_GC, cum, jnp.where(lane < LANE_BETA, gc, sigmoid))
    small_ref[0] = small
    smallt_ref[0] = small.T

    bias = _dot(jnp.concatenate(_split3(cum), axis=1), pmat_ref[...])
    for off, gain_ref, ones_ref, out_ref in ((OFF_Q, qg_ref, oneq_ref, qp_ref),
                                             (OFF_K, kg_ref, onek_ref, kp_ref)):
        p = _dot(xn, w_ref[:, off:off + FOX_PAD])
        for h in range(FOX_HEADS):
            blk = p[:, h * LANES:(h + 1) * LANES]
            ms = jnp.sum(blk * blk, axis=-1, keepdims=True) * (1.0 / FOX_HEAD_DIM)
            nrm = blk * lax.rsqrt(ms + EPS) * gain_ref[...]
            b = bias[:, off + h * LANES:off + (h + 1) * LANES]
            out_ref[0, :, h * LANES:(h + 1) * LANES] = (nrm + b + ones_ref[...]).astype(BF16)
    vp_ref[0] = _dot(xn, w_ref[:, OFF_V:OFF_V + FOX_PAD]).astype(BF16)

    pg = _dot(xn, w_ref[:, OFF_G:OFF_G + 3 * GDN_WIDTH])
    conv_buf[SUBLANES:SUBLANES + tm, :] = pg
    base = SUBLANES - (CONV_WIDTH - 1)
    y = cw_ref[0:1, :] * conv_buf[pl.ds(base, tm), :]
    for k in range(1, CONV_WIDTH):
        y = y + cw_ref[k:k + 1, :] * conv_buf[pl.ds(base + k, tm), :]
    conv_buf[0:SUBLANES, :] = conv_buf[tm:tm + SUBLANES, :]
    y = _silu(y)
    for idx, out_ref in enumerate((gq_ref, gk_ref, gv_ref)):
        for h in range(GDN_HEADS):
            lo = idx * GDN_WIDTH + h * GDN_HEAD_DIM
            blk = y[:, lo:lo + GDN_HEAD_DIM]
            if idx < 2:
                blk = blk * lax.rsqrt(jnp.sum(blk * blk, axis=-1, keepdims=True) + EPS)
            if idx == 0:
                blk = blk * (GDN_HEAD_DIM ** -0.5)
            out_ref[0, :, h * GDN_HEAD_DIM:(h + 1) * GDN_HEAD_DIM] = blk.astype(BF16)

    sgg_ref[0] = _silu(_dot(xn, w_ref[:, OFF_GG:OFF_GG + GDN_WIDTH])).astype(BF16)


def _proj_weights(w_in, fox_q_norm, fox_k_norm, fox_f_bias, gdn_a_log, gdn_dt_bias):
    d = w_in.shape[0]
    fw = FOX_WIDTH
    fq, fk, fv = w_in[:, 0:fw], w_in[:, fw:2 * fw], w_in[:, 2 * fw:3 * fw]
    o = 3 * fw
    ff = w_in[:, o:o + FOX_HEADS]
    o += FOX_HEADS
    gqkv = w_in[:, o:o + 3 * GDN_WIDTH]
    o += 3 * GDN_WIDTH
    ga = w_in[:, o:o + GDN_HEADS]
    gb = w_in[:, o + GDN_HEADS:o + 2 * GDN_HEADS]
    gg = w_in[:, o + 2 * GDN_HEADS:]

    def pad_heads(w):
        w = w.reshape(d, FOX_HEADS, FOX_HEAD_DIM)
        return jnp.pad(w, ((0, 0), (0, 0), (0, LANES - FOX_HEAD_DIM))).reshape(d, FOX_PAD)

    small = jnp.pad(jnp.concatenate([ff, ga, gb], axis=1), ((0, 0), (0, LANES - FOX_HEADS - 2 * GDN_HEADS)))
    w = jnp.concatenate([pad_heads(fq), pad_heads(fk), pad_heads(fv), gqkv, gg, small], axis=1).astype(BF16)

    def lane_row(vals, start):
        return jnp.zeros((1, LANES), F32).at[0, start:start + vals.shape[0]].set(vals.astype(F32))

    qg = lane_row(fox_q_norm * (FOX_HEAD_DIM ** -0.5), 0)
    kg = lane_row(fox_k_norm, 0)
    brow = lane_row(fox_f_bias, LANE_CUM) + lane_row(gdn_dt_bias, LANE_GC)
    alog = lane_row(gdn_a_log, LANE_GC)
    oneq = lane_row(jnp.ones((3,)), BIAS_LANE + 3)
    onek = lane_row(jnp.ones((3,)), BIAS_LANE)

    pm = jnp.zeros((3, LANES, 2, FOX_HEADS, LANES), F32)
    hh = jnp.arange(FOX_HEADS)
    for piece in range(3):
        pm = pm.at[piece, LANE_CUM + hh, 0, hh, BIAS_LANE + piece].set(1.0)
        pm = pm.at[piece, LANE_CUM + hh, 1, hh, BIAS_LANE + 3 + piece].set(-1.0)
    pmat = pm.reshape(3 * LANES, 2 * FOX_PAD).astype(BF16)
    return w, qg, kg, brow, alog, oneq, onek, pmat


def _proj_call(x3d, gain, w_in, fox_q_norm, fox_k_norm, fox_f_bias, gdn_conv, gdn_a_log, gdn_dt_bias):
    b, s, d = x3d.shape
    tm = min(PROJ_TILE_M, s)
    w, qg, kg, brow, alog, oneq, onek, pmat = _proj_weights(
        w_in, fox_q_norm, fox_k_norm, fox_f_bias, gdn_a_log, gdn_dt_bias)
    r = jnp.arange(tm)
    tri = (r[:, None] >= r[None, :])
    btri = tri & ((r[:, None] // GDN_CHUNK) == (r[None, :] // GDN_CHUNK))
    tri = tri.astype(BF16)
    btri = btri.astype(BF16)

    def tok(width, dtype):
        return (pl.BlockSpec((1, tm, width), lambda bi, si: (bi, si, 0)),
                jax.ShapeDtypeStruct((b, s, width), dtype))

    outs = [tok(FOX_PAD, BF16), tok(FOX_PAD, BF16), tok(FOX_PAD, BF16),
            tok(GDN_WIDTH, BF16), tok(GDN_WIDTH, BF16), tok(GDN_WIDTH, BF16),
            tok(GDN_WIDTH, BF16), tok(LANES, F32),
            (pl.BlockSpec((1, LANES, tm), lambda bi, si: (bi, 0, si)),
             jax.ShapeDtypeStruct((b, LANES, s), F32))]
    consts = [gain.reshape(1, d), w, gdn_conv.astype(F32), qg, kg, brow, alog, tri, btri, pmat, oneq, onek]
    return pl.pallas_call(
        functools.partial(_proj_body, tm=tm),
        grid=(b, s // tm),
        in_specs=[pl.BlockSpec((1, tm, d), lambda bi, si: (bi, si, 0))] + [_const_spec(c.shape) for c in consts],
        out_specs=[o[0] for o in outs],
        out_shape=[o[1] for o in outs],
        scratch_shapes=[pltpu.VMEM((1, LANES), F32),
                        pltpu.VMEM((tm + 2 * SUBLANES, 3 * GDN_WIDTH), F32)],
        compiler_params=pltpu.CompilerParams(dimension_semantics=("arbitrary", "arbitrary"),
                                             vmem_limit_bytes=VMEM_LIMIT_BYTES),
        name="mixer_proj",
    )(x3d, *consts)


def _attn_body(q_ref, k_ref, v_ref, o_ref, *, tile):
    i = pl.program_id(2)
    q = q_ref[0]

    def step(j, carry, diagonal):
        m, l, acc = carry
        off = pl.multiple_of(j * tile, tile)
        k = k_ref[0, pl.ds(off, tile), :]
        v = v_ref[0, pl.ds(off, tile), :]
        s = _dot_nt(q, k)
        if diagonal:
            r = lax.broadcasted_iota(jnp.int32, s.shape, 0)
            c = lax.broadcasted_iota(jnp.int32, s.shape, 1)
            s = jnp.where(r >= c, s, -jnp.inf)
        m_new = jnp.maximum(m, jnp.max(s, axis=-1, keepdims=True))
        alpha = jnp.exp(m - m_new)
        p = jnp.exp(s - m_new)
        l = alpha * l + jnp.sum(p, axis=-1, keepdims=True)
        acc = alpha * acc + _dot(p.astype(BF16), v)
        return m_new, l, acc

    init = (jnp.full((tile, 1), -jnp.inf, F32), jnp.zeros((tile, 1), F32), jnp.zeros((tile, LANES), F32))
    carry = lax.fori_loop(0, i, lambda j, c: step(j, c, False), init)
    _, l, acc = step(i, carry, True)
    o_ref[0] = (acc / l).astype(BF16)


def _attn_call(qp, kp, vp):
    b, s, _ = qp.shape
    tile = min(ATTN_TILE, s)
    q_spec = pl.BlockSpec((1, tile, LANES), lambda bi, hi, qi: (bi, qi, hi))
    kv_spec = pl.BlockSpec((1, s, LANES), lambda bi, hi, qi: (bi, 0, hi))
    return pl.pallas_call(
        functools.partial(_attn_body, tile=tile),
        grid=(b, FOX_HEADS, s // tile),
        in_specs=[q_spec, kv_spec, kv_spec],
        out_specs=q_spec,
        out_shape=jax.ShapeDtypeStruct((b, s, FOX_PAD), BF16),
        compiler_params=pltpu.CompilerParams(dimension_semantics=("arbitrary", "arbitrary", "arbitrary"),
                                             vmem_limit_bytes=VMEM_LIMIT_BYTES),
        name="fox_attention",
    )(qp, kp, vp)


def _gdn_body(q_ref, k_ref, v_ref, sgg_ref, small_ref, smallt_ref, gain_ref, y_ref, state_ref):
    c = pl.program_id(1)

    @pl.when(c == 0)
    def _():
        state_ref[...] = jnp.zeros_like(state_ref)

    n = GDN_CHUNK
    sm = small_ref[0]
    smt = smallt_ref[0]
    row = lax.broadcasted_iota(jnp.int32, (n, n), 0)
    col = lax.broadcasted_iota(jnp.int32, (n, n), 1)
    incl = row >= col
    strict = row > col
    eye = (row == col).astype(F32)

    for h in range(GDN_HEADS):
        hs = slice(h * GDN_HEAD_DIM, (h + 1) * GDN_HEAD_DIM)
        gcc = jnp.sum(jnp.where(col == LANE_GC + h, sm, 0.0), axis=1, keepdims=True)
        beta = jnp.sum(jnp.where(col == LANE_BETA + h, sm, 0.0), axis=1, keepdims=True)
        gcr = jnp.sum(jnp.where(row == LANE_GC + h, smt, 0.0), axis=0, keepdims=True)
        g_last = jnp.sum(jnp.where(col[0:1, :] == n - 1, gcr, 0.0), axis=1, keepdims=True)
        decay = jnp.exp(jnp.where(incl, gcc - gcr, -jnp.inf))
        e_g = jnp.exp(gcc)

        q_b = q_ref[0, :, hs]
        k_b = k_ref[0, :, hs]
        k = k_b.astype(F32)
        kb = k * beta
        vb = v_ref[0, :, hs].astype(F32) * beta

        x_pow = jnp.where(strict, -(_dot_nt(kb.astype(BF16), k_b) * decay), 0.0)
        t = eye + x_pow
        for _ in range(NEUMANN_STEPS):
            xb = x_pow.astype(BF16)
            x_pow = _dot(xb, xb)
            t = t + _dot(x_pow.astype(BF16), t.astype(BF16))

        rhs = jnp.concatenate([vb, kb * e_g], axis=1).astype(BF16)
        uw = _dot(t.astype(BF16), rhs)
        u = uw[:, :GDN_HEAD_DIM]
        w = uw[:, GDN_HEAD_DIM:]
        attn = jnp.where(incl, _dot_nt(q_b, k_b) * decay, 0.0)
        kd_t = (k * jnp.exp(g_last - gcc)).T.astype(BF16)
        qd = (q_b.astype(F32) * e_g).astype(BF16)

        st = state_ref[h]
        st_b = st.astype(BF16)
        v_new = u - _dot(w.astype(BF16), st_b)
        v_new_b = v_new.astype(BF16)
        o = _dot(qd, st_b) + _dot(attn.astype(BF16), v_new_b)
        state_ref[h] = st * jnp.exp(g_last) + _dot(kd_t, v_new_b)

        y = _rms_norm(o, gain_ref[...]) * sgg_ref[0, :, hs].astype(F32)
        y_ref[0, :, hs] = y.astype(BF16)


def _gdn_call(gq, gk, gv, sgg, small, small_t, out_gain):
    b, s, _ = gq.shape
    n = GDN_CHUNK
    tok = pl.BlockSpec((1, n, GDN_WIDTH), lambda bi, ci: (bi, ci, 0))
    return pl.pallas_call(
        _gdn_body,
        grid=(b, s // n),
        in_specs=[tok, tok, tok, tok,
                  pl.BlockSpec((1, n, LANES), lambda bi, ci: (bi, ci, 0)),
                  pl.BlockSpec((1, LANES, n), lambda bi, ci: (bi, 0, ci)),
                  _const_spec((1, GDN_HEAD_DIM))],
        out_specs=tok,
        out_shape=jax.ShapeDtypeStruct((b, s, GDN_WIDTH), BF16),
        scratch_shapes=[pltpu.VMEM((GDN_HEADS, GDN_HEAD_DIM, GDN_HEAD_DIM), F32)],
        compiler_params=pltpu.CompilerParams(dimension_semantics=("arbitrary", "arbitrary"),
                                             vmem_limit_bytes=VMEM_LIMIT_BYTES),
        name="gated_delta_rule",
    )(gq, gk, gv, sgg, small, small_t, out_gain.reshape(1, GDN_HEAD_DIM).astype(F32))


def _out_proj_weights(w_out):
    d = w_out.shape[1]
    wf = w_out[:FOX_WIDTH].reshape(FOX_HEADS, FOX_HEAD_DIM, d)
    wf = jnp.pad(wf, ((0, 0), (0, LANES - FOX_HEAD_DIM), (0, 0))).reshape(FOX_PAD, d)
    return wf.astype(BF16), w_out[FOX_WIDTH:].astype(BF16)


def kernel(x, ffn1_norm, ffn1_w_in, ffn1_w_out, mix_norm, w_in, fox_q_norm, fox_k_norm, fox_f_bias,
           gdn_conv, gdn_a_log, gdn_dt_bias, gdn_out_norm, w_out, ffn2_norm, ffn2_w_in, ffn2_w_out):
    b, s, d = x.shape
    depth = ffn1_norm.shape[0]
    h = x.reshape(b * s, d)
    for l in range(depth):
        h = _ffn_call(h, ffn1_norm[l], ffn1_w_in[l], ffn1_w_out[l])
        qp, kp, vp, gq, gk, gv, sgg, small, small_t = _proj_call(
            h.reshape(b, s, d), mix_norm[l], w_in[l], fox_q_norm[l], fox_k_norm[l], fox_f_bias[l],
            gdn_conv[l], gdn_a_log[l], gdn_dt_bias[l])
        y_fox = _attn_call(qp, kp, vp)
        y_gdn = _gdn_call(gq, gk, gv, sgg, small, small_t, gdn_out_norm[l])
        wo_f, wo_g = _out_proj_weights(w_out[l])
        h = _ffn_call(h, ffn2_norm[l], ffn2_w_in[l], ffn2_w_out[l],
                      mix=(y_fox.reshape(b * s, FOX_PAD), y_gdn.reshape(b * s, GDN_WIDTH), wo_f, wo_g))
    return h.reshape(b, s, d)
```

```python
import functools

import jax
import jax.numpy as jnp
from jax import lax
from jax.experimental import pallas as pl
from jax.experimental.pallas import tpu as pltpu

F32 = jnp.float32
BF16 = jnp.bfloat16

D_MODEL = 1024
FOX_HEADS = 8
FOX_HEAD_DIM = 64
FOX_WIDTH = FOX_HEADS * FOX_HEAD_DIM
GDN_HEADS = 4
GDN_HEAD_DIM = 128
GDN_WIDTH = GDN_HEADS * GDN_HEAD_DIM
CONV_WIDTH = 4
D_FF = 2816
EPS = 1e-6

LANES = 128
SUBLANES = 8
VMEM_LIMIT_BYTES = 56 * 1024 * 1024

FOX_PAD = FOX_HEADS * LANES
OFF_Q = 0
OFF_K = OFF_Q + FOX_PAD
OFF_V = OFF_K + FOX_PAD
OFF_G = OFF_V + FOX_WIDTH
OFF_GG = OFF_G + 3 * GDN_WIDTH
OFF_SMALL = OFF_GG + GDN_WIDTH
N_PROJ = OFF_SMALL + LANES

LANE_CUM = 0
LANE_GC = FOX_HEADS
LANE_BETA = LANE_GC + GDN_HEADS
PIECE_LANES = 16

BIAS_LANE = FOX_HEAD_DIM
LOG2E = 1.4426950408889634
BF16_SUBLANES = 16
VT_ROWS = FOX_HEAD_DIM + BF16_SUBLANES

GDN_CHUNK = LANES
NEUMANN_STEPS = 6
GDN_CHUNKS_PER_STEP = 4

FFN_TILE_M = 512
FFN_TILE_F = 256
ATTN_TILE = 512
PROJ_TILE_M = ATTN_TILE
ATTN_HEADS_PER_STEP = 4


def _dot(a, b):
    return jnp.dot(a, b, preferred_element_type=F32)


def _dot_nt(a, b):
    return lax.dot_general(a, b, (((1,), (1,)), ((), ())), preferred_element_type=F32)


def _split3(v):
    hi = v.astype(BF16).astype(F32)
    r1 = v - hi
    mid = r1.astype(BF16).astype(F32)
    lo = (r1 - mid).astype(BF16).astype(F32)
    return hi, mid, lo


def _pack3(v, lane):
    hi, mid, lo = _split3(v)
    packed = jnp.where(lane < PIECE_LANES, hi,
                       jnp.where(lane < 2 * PIECE_LANES, pltpu.roll(mid, PIECE_LANES, 1),
                                 jnp.where(lane < 3 * PIECE_LANES, pltpu.roll(lo, 2 * PIECE_LANES, 1), 0.0)))
    return packed.astype(BF16)


def _unpack3(c):
    return c + pltpu.roll(c, LANES - PIECE_LANES, 1) + pltpu.roll(c, LANES - 2 * PIECE_LANES, 1)


def _rms_norm(x, gain_row):
    ms = jnp.mean(x * x, axis=-1, keepdims=True)
    return x * lax.rsqrt(ms + EPS) * gain_row


def _silu(x):
    return x / (1.0 + jnp.exp(-x))


def _const_spec(shape):
    nd = len(shape)
    return pl.BlockSpec(shape, lambda *_: (0,) * nd, pipeline_mode=pl.Buffered(1))


def _ffn_body(*refs, n_chunks, tile_f, has_mix):
    if has_mix:
        x_ref, yf_ref, yg_ref, wof_ref, wog_ref, g_ref, win_ref, wout_ref, o_ref = refs
    else:
        x_ref, g_ref, win_ref, wout_ref, o_ref = refs
    x = x_ref[...]
    if has_mix:
        x = x + _dot(yf_ref[...], wof_ref[...]) + _dot(yg_ref[...], wog_ref[...])
    xn = _rms_norm(x, g_ref[...]).astype(BF16)
    acc = None
    for c in range(n_chunks):
        h = _dot(xn, win_ref[c])
        act = (_silu(h[:, :tile_f]) * h[:, tile_f:]).astype(BF16)
        part = _dot(act, wout_ref[c])
        acc = part if acc is None else acc + part
    o_ref[...] = x + 0.5 * acc


def _ffn_call(x2d, gain, w_in, w_out, mix=None):
    m, d = x2d.shape
    tm = min(FFN_TILE_M, m)
    tf = FFN_TILE_F
    nc = D_FF // tf
    w_in_r = w_in.reshape(d, 2, nc, tf).transpose(2, 0, 1, 3).reshape(nc, d, 2 * tf).astype(BF16)
    w_out_r = w_out.reshape(nc, tf, d).astype(BF16)
    row = pl.BlockSpec((tm, d), lambda i: (i, 0))
    in_specs = [row]
    args = [x2d]
    if mix is not None:
        y_fox, y_gdn, wo_f, wo_g = mix
        in_specs += [pl.BlockSpec((tm, FOX_WIDTH), lambda i: (i, 0)),
                     pl.BlockSpec((tm, GDN_WIDTH), lambda i: (i, 0)),
                     _const_spec(wo_f.shape), _const_spec(wo_g.shape)]
        args += [y_fox, y_gdn, wo_f, wo_g]
    in_specs += [_const_spec((1, d)), _const_spec(w_in_r.shape), _const_spec(w_out_r.shape)]
    args += [gain.reshape(1, d), w_in_r, w_out_r]
    return pl.pallas_call(
        functools.partial(_ffn_body, n_chunks=nc, tile_f=tf, has_mix=mix is not None),
        grid=(m // tm,),
        in_specs=in_specs,
        out_specs=row,
        out_shape=jax.ShapeDtypeStruct((m, d), F32),
        compiler_params=pltpu.CompilerParams(dimension_semantics=("arbitrary",),
                                             vmem_limit_bytes=VMEM_LIMIT_BYTES),
        name="ffn_mix" if mix is not None else "ffn",
    )(*args)


def _proj_body(x_ref, g_ref, w_ref, cw_ref, qg_ref, kg_ref, brow_ref, alog_ref, tri_ref, btri_ref,
               pmat_ref, oneq_ref, onek_ref,
               qp_ref, kp_ref, vt_ref, gq_ref, gk_ref, gv_ref, sgg_ref, small_ref, smallt_ref,
               cum_carry, conv_buf, *, tm):
    s = pl.program_id(1)

    @pl.when(s == 0)
    def _():
        cum_carry[...] = jnp.zeros_like(cum_carry)
        conv_buf[0:SUBLANES, :] = jnp.zeros((SUBLANES, 3 * GDN_WIDTH), F32)

    xn = _rms_norm(x_ref[0], g_ref[...]).astype(BF16)
    lane = lax.broadcasted_iota(jnp.int32, (tm, LANES), 1)

    z = _dot(xn, w_ref[:, OFF_SMALL:OFF_SMALL + LANES]) + brow_ref[...]
    e = jnp.log1p(jnp.exp(-jnp.abs(z)))
    log_sig = jnp.minimum(z, 0.0) - e
    softplus = jnp.maximum(z, 0.0) + e
    sigmoid = 1.0 / (1.0 + jnp.exp(-z))
    log_decay = -jnp.exp(alog_ref[...]) * softplus
    v3 = _pack3(jnp.where(lane < LANE_GC, log_sig, log_decay), lane)
    cum = _unpack3(_dot(tri_ref[...], v3)) + cum_carry[...]
    gc = _unpack3(_dot(btri_ref[...], v3))
    cum_carry[...] = cum[tm - 1:tm, :]
    small = jnp.where(lane < LANE_GC, cum, jnp.where(lane < LANE_BETA, gc, sigmoid))
    small_ref[0] = small
    smallt_ref[0] = small.T

    bias = _dot(_pack3(cum * LOG2E, lane), pmat_ref[...])
    for off, gain_ref, ones_ref, out_ref in ((OFF_Q, qg_ref, oneq_ref, qp_ref),
                                             (OFF_K, kg_ref, onek_ref, kp_ref)):
        p = _dot(xn, w_ref[:, off:off + FOX_PAD])
        for h in range(FOX_HEADS):
            blk = p[:, h * LANES:(h + 1) * LANES]
            ms = jnp.sum(blk * blk, axis=-1, keepdims=True) * (1.0 / FOX_HEAD_DIM)
            nrm = blk * lax.rsqrt(ms + EPS) * gain_ref[...]
            b = bias[:, off + h * LANES:off + (h + 1) * LANES]
            out_ref[0, :, h * LANES:(h + 1) * LANES] = (nrm + b + ones_ref[...]).astype(BF16)

    v_t = _dot(xn, w_ref[:, OFF_V:OFF_V + FOX_WIDTH]).T.astype(BF16)
    for h in range(FOX_HEADS):
        vt_ref[0, 0, h * VT_ROWS:h * VT_ROWS + FOX_HEAD_DIM, :] = v_t[h * FOX_HEAD_DIM:(h + 1) * FOX_HEAD_DIM, :]
        vt_ref[0, 0, h * VT_ROWS + FOX_HEAD_DIM:(h + 1) * VT_ROWS, :] = jnp.ones((VT_ROWS - FOX_HEAD_DIM, tm), BF16)

    pg = _dot(xn, w_ref[:, OFF_G:OFF_G + 3 * GDN_WIDTH])
    conv_buf[SUBLANES:SUBLANES + tm, :] = pg
    base = SUBLANES - (CONV_WIDTH - 1)
    y = cw_ref[0:1, :] * conv_buf[base:base + tm, :]
    for k in range(1, CONV_WIDTH):
        y = y + cw_ref[k:k + 1, :] * conv_buf[base + k:base + k + tm, :]
    conv_buf[0:SUBLANES, :] = conv_buf[tm:tm + SUBLANES, :]
    y = _silu(y)
    for idx, out_ref in enumerate((gq_ref, gk_ref, gv_ref)):
        for h in range(GDN_HEADS):
            lo = idx * GDN_WIDTH + h * GDN_HEAD_DIM
            blk = y[:, lo:lo + GDN_HEAD_DIM]
            if idx < 2:
                blk = blk * lax.rsqrt(jnp.sum(blk * blk, axis=-1, keepdims=True) + EPS)
            if idx == 0:
                blk = blk * (GDN_HEAD_DIM ** -0.5)
            out_ref[0, :, h * GDN_HEAD_DIM:(h + 1) * GDN_HEAD_DIM] = blk.astype(BF16)

    sgg_ref[0] = _silu(_dot(xn, w_ref[:, OFF_GG:OFF_GG + GDN_WIDTH])).astype(BF16)


def _proj_weights(w_in, fox_q_norm, fox_k_norm, fox_f_bias, gdn_a_log, gdn_dt_bias):
    d = w_in.shape[0]
    fw = FOX_WIDTH
    fq, fk, fv = w_in[:, 0:fw], w_in[:, fw:2 * fw], w_in[:, 2 * fw:3 * fw]
    o = 3 * fw
    ff = w_in[:, o:o + FOX_HEADS]
    o += FOX_HEADS
    gqkv = w_in[:, o:o + 3 * GDN_WIDTH]
    o += 3 * GDN_WIDTH
    ga = w_in[:, o:o + GDN_HEADS]
    gb = w_in[:, o + GDN_HEADS:o + 2 * GDN_HEADS]
    gg = w_in[:, o + 2 * GDN_HEADS:]

    def pad_heads(w):
        w = w.reshape(d, FOX_HEADS, FOX_HEAD_DIM)
        return jnp.pad(w, ((0, 0), (0, 0), (0, LANES - FOX_HEAD_DIM))).reshape(d, FOX_PAD)

    small = jnp.pad(jnp.concatenate([ff, ga, gb], axis=1), ((0, 0), (0, LANES - FOX_HEADS - 2 * GDN_HEADS)))
    w = jnp.concatenate([pad_heads(fq), pad_heads(fk), fv, gqkv, gg, small], axis=1).astype(BF16)

    def lane_row(vals, start):
        return jnp.zeros((1, LANES), F32).at[0, start:start + vals.shape[0]].set(vals.astype(F32))

    qg = lane_row(fox_q_norm * (FOX_HEAD_DIM ** -0.5 * LOG2E), 0)
    kg = lane_row(fox_k_norm, 0)
    brow = lane_row(fox_f_bias, LANE_CUM) + lane_row(gdn_dt_bias, LANE_GC)
    alog = lane_row(gdn_a_log, LANE_GC)
    oneq = lane_row(jnp.ones((3,)), BIAS_LANE + 3)
    onek = lane_row(jnp.ones((3,)), BIAS_LANE)

    pm = jnp.zeros((LANES, 2, FOX_HEADS, LANES), F32)
    hh = jnp.arange(FOX_HEADS)
    for piece in range(3):
        src = piece * PIECE_LANES + LANE_CUM + hh
        pm = pm.at[src, 0, hh, BIAS_LANE + piece].set(1.0)
        pm = pm.at[src, 1, hh, BIAS_LANE + 3 + piece].set(-1.0)
    pmat = pm.reshape(LANES, 2 * FOX_PAD).astype(BF16)
    return w, qg, kg, brow, alog, oneq, onek, pmat


def _proj_call(x3d, gain, w_in, fox_q_norm, fox_k_norm, fox_f_bias, gdn_conv, gdn_a_log, gdn_dt_bias):
    b, s, d = x3d.shape
    tm = min(PROJ_TILE_M, s)
    w, qg, kg, brow, alog, oneq, onek, pmat = _proj_weights(
        w_in, fox_q_norm, fox_k_norm, fox_f_bias, gdn_a_log, gdn_dt_bias)
    r = jnp.arange(tm)
    tri = (r[:, None] >= r[None, :])
    btri = tri & ((r[:, None] // GDN_CHUNK) == (r[None, :] // GDN_CHUNK))
    tri = tri.astype(BF16)
    btri = btri.astype(BF16)

    def tok(width, dtype):
        return (pl.BlockSpec((1, tm, width), lambda bi, si: (bi, si, 0)),
                jax.ShapeDtypeStruct((b, s, width), dtype))

    outs = [tok(FOX_PAD, BF16), tok(FOX_PAD, BF16),
            (pl.BlockSpec((1, 1, FOX_HEADS * VT_ROWS, tm), lambda bi, si: (bi, si, 0, 0)),
             jax.ShapeDtypeStruct((b, s // tm, FOX_HEADS * VT_ROWS, tm), BF16)),
            tok(GDN_WIDTH, BF16), tok(GDN_WIDTH, BF16), tok(GDN_WIDTH, BF16),
            tok(GDN_WIDTH, BF16), tok(LANES, F32),
            (pl.BlockSpec((1, LANES, tm), lambda bi, si: (bi, 0, si)),
             jax.ShapeDtypeStruct((b, LANES, s), F32))]
    consts = [gain.reshape(1, d), w, gdn_conv.astype(F32), qg, kg, brow, alog, tri, btri, pmat, oneq, onek]
    return pl.pallas_call(
        functools.partial(_proj_body, tm=tm),
        grid=(b, s // tm),
        in_specs=[pl.BlockSpec((1, tm, d), lambda bi, si: (bi, si, 0))] + [_const_spec(c.shape) for c in consts],
        out_specs=[o[0] for o in outs],
        out_shape=[o[1] for o in outs],
        scratch_shapes=[pltpu.VMEM((1, LANES), F32),
                        pltpu.VMEM((tm + SUBLANES, 3 * GDN_WIDTH), F32)],
        compiler_params=pltpu.CompilerParams(dimension_semantics=("arbitrary", "arbitrary"),
                                             vmem_limit_bytes=VMEM_LIMIT_BYTES),
        name="mixer_proj",
    )(x3d, *consts)


def _attn_body(q_ref, k_ref, vt_ref, o_ref, *, tile):
    i = pl.program_id(2)
    nh = ATTN_HEADS_PER_STEP
    dh = FOX_HEAD_DIM

    def logits(hh, j):
        off = pl.multiple_of(j * tile, tile)
        q = q_ref[0, :, hh * LANES:(hh + 1) * LANES]
        k = k_ref[0, pl.ds(off, tile), hh * LANES:(hh + 1) * LANES]
        return _dot_nt(k, q)

    def step(j, carry, diagonal):
        ahead = 2
        s_all = {hh: logits(hh, j) for hh in range(min(ahead, nh))}
        out = []
        for hh in range(nh):
            if hh + ahead < nh:
                s_all[hh + ahead] = logits(hh + ahead, j)
            m, acc = carry[hh]
            s = s_all.pop(hh)
            if diagonal:
                key = lax.broadcasted_iota(jnp.int32, s.shape, 0)
                qry = lax.broadcasted_iota(jnp.int32, s.shape, 1)
                s = jnp.where(qry >= key, s, -jnp.inf)
            m_new = jnp.maximum(m, jnp.max(s, axis=0, keepdims=True))
            p = jnp.exp2(s - m_new).astype(BF16)
            vt = vt_ref[0, j, hh * VT_ROWS:(hh + 1) * VT_ROWS, :]
            acc = jnp.exp2(m - m_new) * acc + _dot(vt, p)
            out.append((m_new, acc))
        return tuple(out)

    init = tuple((jnp.full((1, tile), -jnp.inf, F32), jnp.zeros((VT_ROWS, tile), F32)) for _ in range(nh))
    carry = lax.fori_loop(0, i, lambda j, c: step(j, c, False), init)
    carry = step(i, carry, True)
    out_t = jnp.concatenate([acc[:dh] / acc[dh:dh + 1] for _, acc in carry], axis=0)
    o_ref[0] = out_t.T.astype(BF16)


def _attn_call(qp, kp, vt):
    b, s, _ = qp.shape
    tile = min(ATTN_TILE, s)
    nh = ATTN_HEADS_PER_STEP
    return pl.pallas_call(
        functools.partial(_attn_body, tile=tile),
        grid=(b, FOX_HEADS // nh, s // tile),
        in_specs=[pl.BlockSpec((1, tile, nh * LANES), lambda bi, hi, qi: (bi, qi, hi)),
                  pl.BlockSpec((1, s, nh * LANES), lambda bi, hi, qi: (bi, 0, hi)),
                  pl.BlockSpec((1, s // tile, nh * VT_ROWS, tile), lambda bi, hi, qi: (bi, 0, hi, 0))],
        out_specs=pl.BlockSpec((1, tile, nh * FOX_HEAD_DIM), lambda bi, hi, qi: (bi, qi, hi)),
        out_shape=jax.ShapeDtypeStruct((b, s, FOX_WIDTH), BF16),
        compiler_params=pltpu.CompilerParams(dimension_semantics=("arbitrary", "arbitrary", "arbitrary"),
                                             vmem_limit_bytes=VMEM_LIMIT_BYTES),
        name="fox_attention",
    )(qp, kp, vt)


def _gdn_body(q_ref, k_ref, v_ref, sgg_ref, small_ref, smallt_ref, gain_ref, y_ref, state_ref, *, n_chunks):
    @pl.when(pl.program_id(1) == 0)
    def _():
        state_ref[...] = jnp.zeros_like(state_ref)

    n = GDN_CHUNK
    row = lax.broadcasted_iota(jnp.int32, (n, n), 0)
    col = lax.broadcasted_iota(jnp.int32, (n, n), 1)
    incl = row >= col
    strict = row > col
    eye = (row == col).astype(F32)

    pairs = [(c, h) for c in range(n_chunks) for h in range(GDN_HEADS)]
    pre = []
    for c, h in pairs:
        rows = slice(c * n, (c + 1) * n)
        hs = slice(h * GDN_HEAD_DIM, (h + 1) * GDN_HEAD_DIM)
        sm = small_ref[0, rows, :]
        smt = smallt_ref[0, :, rows]
        gcc = jnp.sum(jnp.where(col == LANE_GC + h, sm, 0.0), axis=1, keepdims=True)
        beta = jnp.sum(jnp.where(col == LANE_BETA + h, sm, 0.0), axis=1, keepdims=True)
        gcr = jnp.sum(jnp.where(row == LANE_GC + h, smt, 0.0), axis=0, keepdims=True)
        g_last = jnp.sum(jnp.where(col[0:1, :] == n - 1, gcr, 0.0), axis=1, keepdims=True)
        k_b = k_ref[0, rows, hs]
        k = k_b.astype(F32)
        kb = k * beta
        pre.append(dict(rows=rows, hs=hs, gcc=gcc, g_last=g_last, k_b=k_b, k=k, kb=kb,
                        q_b=q_ref[0, rows, hs], vb=v_ref[0, rows, hs].astype(F32) * beta,
                        decay=jnp.exp(jnp.where(incl, gcc - gcr, -jnp.inf)), e_g=jnp.exp(gcc)))

    x_pow = [jnp.where(strict, -(_dot_nt(p["kb"].astype(BF16), p["k_b"]) * p["decay"]), 0.0) for p in pre]
    t = [eye + x for x in x_pow]
    for _ in range(NEUMANN_STEPS):
        x_pow = [_dot(x.astype(BF16), x.astype(BF16)) for x in x_pow]
        t = [ti + _dot(x.astype(BF16), ti.astype(BF16)) for x, ti in zip(x_pow, t)]

    uw = [_dot(ti.astype(BF16), jnp.concatenate([p["vb"], p["kb"] * p["e_g"]], axis=1).astype(BF16))
          for ti, p in zip(t, pre)]
    attn = [jnp.where(incl, _dot_nt(p["q_b"], p["k_b"]) * p["decay"], 0.0).astype(BF16) for p in pre]

    states = [state_ref[h] for h in range(GDN_HEADS)]
    for c in range(n_chunks):
        idx = [c * GDN_HEADS + h for h in range(GDN_HEADS)]
        st_b = [st.astype(BF16) for st in states]
        v_new = [(uw[i][:, :GDN_HEAD_DIM] - _dot(uw[i][:, GDN_HEAD_DIM:].astype(BF16), st_b[h])).astype(BF16)
                 for h, i in enumerate(idx)]
        new_states = []
        for h, i in enumerate(idx):
            p = pre[i]
            kd_t = (p["k"] * jnp.exp(p["g_last"] - p["gcc"])).T.astype(BF16)
            new_states.append(states[h] * jnp.exp(p["g_last"]) + _dot(kd_t, v_new[h]))
        for h, i in enumerate(idx):
            p = pre[i]
            qd = (p["q_b"].astype(F32) * p["e_g"]).astype(BF16)
            o = _dot(qd, st_b[h]) + _dot(attn[i], v_new[h])
            y = _rms_norm(o, gain_ref[...]) * sgg_ref[0, p["rows"], p["hs"]].astype(F32)
            y_ref[0, p["rows"], p["hs"]] = y.astype(BF16)
        states = new_states

    for h in range(GDN_HEADS):
        state_ref[h] = states[h]


def _gdn_call(gq, gk, gv, sgg, small, small_t, out_gain):
    b, s, _ = gq.shape
    n_chunks = min(GDN_CHUNKS_PER_STEP, s // GDN_CHUNK)
    ts = n_chunks * GDN_CHUNK
    tok = pl.BlockSpec((1, ts, GDN_WIDTH), lambda bi, ci: (bi, ci, 0))
    return pl.pallas_call(
        functools.partial(_gdn_body, n_chunks=n_chunks),
        grid=(b, s // ts),
        in_specs=[tok, tok, tok, tok,
                  pl.BlockSpec((1, ts, LANES), lambda bi, ci: (bi, ci, 0)),
                  pl.BlockSpec((1, LANES, ts), lambda bi, ci: (bi, 0, ci)),
                  _const_spec((1, GDN_HEAD_DIM))],
        out_specs=tok,
        out_shape=jax.ShapeDtypeStruct((b, s, GDN_WIDTH), BF16),
        scratch_shapes=[pltpu.VMEM((GDN_HEADS, GDN_HEAD_DIM, GDN_HEAD_DIM), F32)],
        compiler_params=pltpu.CompilerParams(dimension_semantics=("arbitrary", "arbitrary"),
                                             vmem_limit_bytes=VMEM_LIMIT_BYTES),
        name="gated_delta_rule",
    )(gq, gk, gv, sgg, small, small_t, out_gain.reshape(1, GDN_HEAD_DIM).astype(F32))


def kernel(x, ffn1_norm, ffn1_w_in, ffn1_w_out, mix_norm, w_in, fox_q_norm, fox_k_norm, fox_f_bias,
           gdn_conv, gdn_a_log, gdn_dt_bias, gdn_out_norm, w_out, ffn2_norm, ffn2_w_in, ffn2_w_out):
    b, s, d = x.shape
    depth = ffn1_norm.shape[0]
    h = x.reshape(b * s, d)
    for l in range(depth):
        h = _ffn_call(h, ffn1_norm[l], ffn1_w_in[l], ffn1_w_out[l])
        qp, kp, vt, gq, gk, gv, sgg, small, small_t = _proj_call(
            h.reshape(b, s, d), mix_norm[l], w_in[l], fox_q_norm[l], fox_k_norm[l], fox_f_bias[l],
            gdn_conv[l], gdn_a_log[l], gdn_dt_bias[l])
        y_fox = _attn_call(qp, kp, vt)
        y_gdn = _gdn_call(gq, gk, gv, sgg, small, small_t, gdn_out_norm[l])
        wo = w_out[l].astype(BF16)
        h = _ffn_call(h, ffn2_norm[l], ffn2_w_in[l], ffn2_w_out[l],
                      mix=(y_fox.reshape(b * s, FOX_WIDTH), y_gdn.reshape(b * s, GDN_WIDTH),
                           wo[:FOX_WIDTH], wo[FOX_WIDTH:]))
    return h.reshape(b, s, d)
```

```python
import functools

import jax
import jax.numpy as jnp
from jax import lax
from jax.experimental import pallas as pl
from jax.experimental.pallas import tpu as pltpu

F32 = jnp.float32
BF16 = jnp.bfloat16

D_MODEL = 1024
FOX_HEADS = 8
FOX_HEAD_DIM = 64
FOX_WIDTH = FOX_HEADS * FOX_HEAD_DIM
GDN_HEADS = 4
GDN_HEAD_DIM = 128
GDN_WIDTH = GDN_HEADS * GDN_HEAD_DIM
CONV_WIDTH = 4
D_FF = 2816
EPS = 1e-6

LANES = 128
SUBLANES = 8
VMEM_LIMIT_BYTES = 56 * 1024 * 1024

FOX_PAD = FOX_HEADS * LANES
OFF_Q = 0
OFF_K = OFF_Q + FOX_PAD
OFF_V = OFF_K + FOX_PAD
OFF_G = OFF_V + FOX_WIDTH
OFF_GG = OFF_G + 3 * GDN_WIDTH
OFF_SMALL = OFF_GG + GDN_WIDTH
N_PROJ = OFF_SMALL + LANES

LANE_CUM = 0
LANE_GC = FOX_HEADS
LANE_BETA = LANE_GC + GDN_HEADS
GATE_ROWS = 16
PIECE_LANES = GATE_ROWS

BIAS_LANE = FOX_HEAD_DIM
LOG2E = 1.4426950408889634
BF16_SUBLANES = 16
VT_ROWS = FOX_HEAD_DIM + BF16_SUBLANES

GDN_CHUNK = LANES
NEUMANN_STEPS = 6
GDN_CHUNKS_PER_STEP = 4

FFN_TILE_M = 512
FFN_TILE_F = 256
ATTN_TILE = 512
PROJ_TILE_M = ATTN_TILE
ATTN_HEADS_PER_STEP = 8


def _dot(a, b):
    return jnp.dot(a, b, preferred_element_type=F32)


def _dot_nt(a, b):
    return lax.dot_general(a, b, (((1,), (1,)), ((), ())), preferred_element_type=F32)


def _keep_bf16_bits(v):
    bits = lax.bitcast_convert_type(v, jnp.uint32) & jnp.uint32(0xFFFF0000)
    return lax.bitcast_convert_type(bits, F32)


def _split3(v):
    hi = _keep_bf16_bits(v)
    r1 = v - hi
    mid = _keep_bf16_bits(r1)
    lo = _keep_bf16_bits(r1 - mid)
    return hi, mid, lo


def _pack3(v, lane, one_lane=False):
    hi, mid, lo = _split3(v)
    tail = jnp.where(lane == 3 * PIECE_LANES, 1.0, 0.0) if one_lane else 0.0
    packed = jnp.where(lane < PIECE_LANES, hi,
                       jnp.where(lane < 2 * PIECE_LANES, pltpu.roll(mid, PIECE_LANES, 1),
                                 jnp.where(lane < 3 * PIECE_LANES, pltpu.roll(lo, 2 * PIECE_LANES, 1), tail)))
    return packed.astype(BF16)


def _rms_norm(x, gain_row):
    ms = jnp.mean(x * x, axis=-1, keepdims=True)
    return x * lax.rsqrt(ms + EPS) * gain_row


def _silu(x):
    return x / (1.0 + jnp.exp(-x))


def _const_spec(shape):
    nd = len(shape)
    return pl.BlockSpec(shape, lambda *_: (0,) * nd, pipeline_mode=pl.Buffered(1))


def _ffn_body(*refs, n_chunks, tile_f, has_mix):
    if has_mix:
        x_ref, yf_ref, yg_ref, wof_ref, wog_ref, g_ref, win_ref, wout_ref, o_ref = refs
    else:
        x_ref, g_ref, win_ref, wout_ref, o_ref = refs
    x = x_ref[...]
    if has_mix:
        x = x + _dot(yf_ref[...], wof_ref[...]) + _dot(yg_ref[...], wog_ref[...])
    xn = _rms_norm(x, g_ref[...]).astype(BF16)
    acc = None
    for c in range(n_chunks):
        cols = slice(c * tile_f, (c + 1) * tile_f)
        gate = _dot(xn, win_ref[:, cols])
        up = _dot(xn, win_ref[:, D_FF + c * tile_f:D_FF + (c + 1) * tile_f])
        act = (_silu(gate) * up).astype(BF16)
        part = _dot(act, wout_ref[cols, :])
        acc = part if acc is None else acc + part
    o_ref[...] = x + 0.5 * acc


def _ffn_call(x2d, gain, w_in, w_out, mix=None):
    m, d = x2d.shape
    tm = min(FFN_TILE_M, m)
    tf = FFN_TILE_F
    nc = D_FF // tf
    w_in_r = w_in.astype(BF16)
    w_out_r = w_out.astype(BF16)
    row = pl.BlockSpec((tm, d), lambda i: (i, 0))
    in_specs = [row]
    args = [x2d]
    if mix is not None:
        y_fox, y_gdn, wo_f, wo_g = mix
        in_specs += [pl.BlockSpec((tm, FOX_WIDTH), lambda i: (i, 0)),
                     pl.BlockSpec((tm, GDN_WIDTH), lambda i: (i, 0)),
                     _const_spec(wo_f.shape), _const_spec(wo_g.shape)]
        args += [y_fox, y_gdn, wo_f, wo_g]
    in_specs += [_const_spec((1, d)), _const_spec(w_in_r.shape), _const_spec(w_out_r.shape)]
    args += [gain.reshape(1, d), w_in_r, w_out_r]
    return pl.pallas_call(
        functools.partial(_ffn_body, n_chunks=nc, tile_f=tf, has_mix=mix is not None),
        grid=(m // tm,),
        in_specs=in_specs,
        out_specs=row,
        out_shape=jax.ShapeDtypeStruct((m, d), F32),
        compiler_params=pltpu.CompilerParams(dimension_semantics=("arbitrary",),
                                             vmem_limit_bytes=VMEM_LIMIT_BYTES),
        name="ffn_mix" if mix is not None else "ffn",
    )(*args)


def _proj_body(x_ref, g_ref, w_ref, cw_ref, qg_ref, browt_ref, alogt_ref, triu_ref, pmat_ref,
               qp_ref, kp_ref, vt_ref, gq_ref, gk_ref, gv_ref, sgg_ref, small_ref, smallt_ref,
               cum_carry, conv_buf, pack_buf, *, tm):
    s = pl.program_id(1)

    @pl.when(s == 0)
    def _():
        cum_carry[...] = jnp.zeros_like(cum_carry)
        conv_buf[0:SUBLANES, :] = jnp.zeros((SUBLANES, 3 * GDN_WIDTH), F32)

    xn = _rms_norm(x_ref[0], g_ref[...]).astype(BF16)
    lane = lax.broadcasted_iota(jnp.int32, (tm, LANES), 1)

    z = _dot(xn, w_ref[:, OFF_SMALL:OFF_SMALL + LANES])
    pg = _dot(xn, w_ref[:, OFF_G:OFF_G + 3 * GDN_WIDTH])

    zt = z.T[0:GATE_ROWS] + browt_ref[...]
    grow = lax.broadcasted_iota(jnp.int32, (GATE_ROWS, tm), 0)
    e = jnp.log1p(jnp.exp(-jnp.abs(zt)))
    log_sig = jnp.minimum(zt, 0.0) - e
    softplus = jnp.maximum(zt, 0.0) + e
    sigmoid = 1.0 / (1.0 + jnp.exp(-zt))
    log_decay = -jnp.exp(alogt_ref[...]) * softplus
    pieces = jnp.concatenate(_split3(jnp.where(grow < LANE_GC, log_sig, log_decay)), axis=0).astype(BF16)
    c3 = _dot(pieces, triu_ref[...])
    c = c3[0:GATE_ROWS] + c3[GATE_ROWS:2 * GATE_ROWS] + c3[2 * GATE_ROWS:]
    cum_t = c[:, :tm] + jnp.concatenate([cum_carry[...]] * (tm // LANES), axis=1)
    pq = _dot(xn, w_ref[:, OFF_Q:OFF_Q + FOX_PAD])
    cum_carry[...] = jnp.broadcast_to(cum_t[:, tm - 1:tm], (GATE_ROWS, LANES))
    small_t = jnp.where(grow < LANE_GC, cum_t, jnp.where(grow < LANE_BETA, c[:, tm:], sigmoid))
    smallt_ref[0] = small_t
    small = jnp.concatenate([small_t, jnp.zeros((LANES - GATE_ROWS, tm), F32)], axis=0).T
    small_ref[0] = small

    pack_buf[...] = _pack3(small * LOG2E, lane, one_lane=True)
    bias = _dot(pack_buf[...], pmat_ref[...])
    pk = _dot(xn, w_ref[:, OFF_K:OFF_K + FOX_PAD])
    pv = _dot(xn, w_ref[:, OFF_V:OFF_V + FOX_WIDTH])
    pgg = _dot(xn, w_ref[:, OFF_GG:OFF_GG + GDN_WIDTH])
    for off, p, out_ref in ((OFF_Q, pq, qp_ref), (OFF_K, pk, kp_ref)):
        for h in range(FOX_HEADS):
            blk = p[:, h * LANES:(h + 1) * LANES]
            ms = jnp.sum(blk * blk, axis=-1, keepdims=True) * (1.0 / FOX_HEAD_DIM)
            nrm = blk * lax.rsqrt(ms + EPS)
            if off == OFF_Q:
                nrm = nrm * qg_ref[...]
            out_ref[0, :, h * LANES:(h + 1) * LANES] = (
                nrm + bias[:, off + h * LANES:off + (h + 1) * LANES]).astype(BF16)

    v_t = pv.T.astype(BF16)
    for h in range(FOX_HEADS):
        vt_ref[0, 0, h * VT_ROWS:h * VT_ROWS + FOX_HEAD_DIM, :] = v_t[h * FOX_HEAD_DIM:(h + 1) * FOX_HEAD_DIM, :]
        vt_ref[0, 0, h * VT_ROWS + FOX_HEAD_DIM:(h + 1) * VT_ROWS, :] = jnp.ones((VT_ROWS - FOX_HEAD_DIM, tm), BF16)

    conv_buf[SUBLANES:SUBLANES + tm, :] = pg
    base = SUBLANES - (CONV_WIDTH - 1)
    y = cw_ref[0:1, :] * conv_buf[base:base + tm, :]
    for k in range(1, CONV_WIDTH):
        y = y + cw_ref[k:k + 1, :] * conv_buf[base + k:base + k + tm, :]
    conv_buf[0:SUBLANES, :] = conv_buf[tm:tm + SUBLANES, :]
    y = _silu(y)
    for idx, out_ref in enumerate((gq_ref, gk_ref, gv_ref)):
        for h in range(GDN_HEADS):
            lo = idx * GDN_WIDTH + h * GDN_HEAD_DIM
            blk = y[:, lo:lo + GDN_HEAD_DIM]
            if idx < 2:
                blk = blk * lax.rsqrt(jnp.sum(blk * blk, axis=-1, keepdims=True) + EPS)
            if idx == 0:
                blk = blk * (GDN_HEAD_DIM ** -0.5)
            out_ref[0, :, h * GDN_HEAD_DIM:(h + 1) * GDN_HEAD_DIM] = blk.astype(BF16)

    sgg_ref[0] = _silu(pgg).astype(BF16)


def _proj_weights(w_in, fox_q_norm, fox_k_norm, fox_f_bias, gdn_a_log, gdn_dt_bias):
    d = w_in.shape[0]
    fw = FOX_WIDTH
    fq, fk, fv = w_in[:, 0:fw], w_in[:, fw:2 * fw], w_in[:, 2 * fw:3 * fw]
    o = 3 * fw
    ff = w_in[:, o:o + FOX_HEADS]
    o += FOX_HEADS
    gqkv = w_in[:, o:o + 3 * GDN_WIDTH]
    o += 3 * GDN_WIDTH
    ga = w_in[:, o:o + GDN_HEADS]
    gb = w_in[:, o + GDN_HEADS:o + 2 * GDN_HEADS]
    gg = w_in[:, o + 2 * GDN_HEADS:]

    def pad_heads(w):
        w = w.reshape(d, FOX_HEADS, FOX_HEAD_DIM)
        return jnp.pad(w, ((0, 0), (0, 0), (0, LANES - FOX_HEAD_DIM))).reshape(d, FOX_PAD)

    small = jnp.pad(jnp.concatenate([ff, ga, gb], axis=1), ((0, 0), (0, LANES - FOX_HEADS - 2 * GDN_HEADS)))
    w = jnp.concatenate([pad_heads(fq), pad_heads(fk), fv, gqkv, gg, small], axis=1).astype(BF16)

    def lane_row(vals, start):
        return jnp.zeros((1, LANES), F32).at[0, start:start + vals.shape[0]].set(vals.astype(F32))

    qg = lane_row(fox_q_norm * fox_k_norm * (FOX_HEAD_DIM ** -0.5 * LOG2E), 0)
    brow = lane_row(fox_f_bias, LANE_CUM) + lane_row(gdn_dt_bias, LANE_GC)
    alog = lane_row(gdn_a_log, LANE_GC)

    pm = jnp.zeros((LANES, 2, FOX_HEADS, LANES), F32)
    hh = jnp.arange(FOX_HEADS)
    for piece in range(3):
        src = piece * PIECE_LANES + LANE_CUM + hh
        pm = pm.at[src, 0, hh, BIAS_LANE + piece].set(1.0)
        pm = pm.at[src, 1, hh, BIAS_LANE + 3 + piece].set(-1.0)
        pm = pm.at[3 * PIECE_LANES, 0, hh, BIAS_LANE + 3 + piece].set(1.0)
        pm = pm.at[3 * PIECE_LANES, 1, hh, BIAS_LANE + piece].set(1.0)
    pmat = pm.reshape(LANES, 2 * FOX_PAD).astype(BF16)
    return w, qg, brow, alog, pmat


def _proj_call(x3d, gain, w_in, fox_q_norm, fox_k_norm, fox_f_bias, gdn_conv, gdn_a_log, gdn_dt_bias):
    b, s, d = x3d.shape
    tm = min(PROJ_TILE_M, s)
    w, qg, brow, alog, pmat = _proj_weights(w_in, fox_q_norm, fox_k_norm, fox_f_bias, gdn_a_log, gdn_dt_bias)
    r = jnp.arange(tm)
    upper = r[:, None] <= r[None, :]
    same_chunk = (r[:, None] // GDN_CHUNK) == (r[None, :] // GDN_CHUNK)
    triu = jnp.concatenate([upper, upper & same_chunk], axis=1).astype(BF16)
    brow_t = jnp.broadcast_to(brow[0, :GATE_ROWS, None], (GATE_ROWS, tm))
    alog_t = jnp.broadcast_to(alog[0, :GATE_ROWS, None], (GATE_ROWS, tm))

    def tok(width, dtype):
        return (pl.BlockSpec((1, tm, width), lambda bi, si: (bi, si, 0)),
                jax.ShapeDtypeStruct((b, s, width), dtype))

    outs = [tok(FOX_PAD, BF16), tok(FOX_PAD, BF16),
            (pl.BlockSpec((1, 1, FOX_HEADS * VT_ROWS, tm), lambda bi, si: (bi, si, 0, 0)),
             jax.ShapeDtypeStruct((b, s // tm, FOX_HEADS * VT_ROWS, tm), BF16)),
            tok(GDN_WIDTH, BF16), tok(GDN_WIDTH, BF16), tok(GDN_WIDTH, BF16),
            tok(GDN_WIDTH, BF16), tok(LANES, F32),
            (pl.BlockSpec((1, GATE_ROWS, tm), lambda bi, si: (bi, 0, si)),
             jax.ShapeDtypeStruct((b, GATE_ROWS, s), F32))]
    consts = [gain.reshape(1, d), w, gdn_conv.astype(F32), qg, brow_t, alog_t, triu, pmat]
    return pl.pallas_call(
        functools.partial(_proj_body, tm=tm),
        grid=(b, s // tm),
        in_specs=[pl.BlockSpec((1, tm, d), lambda bi, si: (bi, si, 0))] + [_const_spec(c.shape) for c in consts],
        out_specs=[o[0] for o in outs],
        out_shape=[o[1] for o in outs],
        scratch_shapes=[pltpu.VMEM((GATE_ROWS, LANES), F32),
                        pltpu.VMEM((tm + SUBLANES, 3 * GDN_WIDTH), F32),
                        pltpu.VMEM((tm, LANES), BF16)],
        compiler_params=pltpu.CompilerParams(dimension_semantics=("arbitrary", "arbitrary"),
                                             vmem_limit_bytes=VMEM_LIMIT_BYTES),
        name="mixer_proj",
    )(x3d, *consts)


def _attn_body(q_ref, k_ref, vt_ref, o_ref, *, tile):
    i = pl.program_id(2)
    nh = ATTN_HEADS_PER_STEP
    dh = FOX_HEAD_DIM

    def logits(hh, j):
        off = pl.multiple_of(j * tile, tile)
        q = q_ref[0, :, hh * LANES:(hh + 1) * LANES]
        k = k_ref[0, pl.ds(off, tile), hh * LANES:(hh + 1) * LANES]
        return _dot_nt(k, q)

    def step(j, carry, diagonal):
        ahead = 2
        s_all = {hh: logits(hh, j) for hh in range(min(ahead, nh))}
        out = []
        for hh in range(nh):
            if hh + ahead < nh:
                s_all[hh + ahead] = logits(hh + ahead, j)
            m, acc = carry[hh]
            s = s_all.pop(hh)
            if diagonal:
                key = lax.broadcasted_iota(jnp.int32, s.shape, 0)
                qry = lax.broadcasted_iota(jnp.int32, s.shape, 1)
                s = jnp.where(qry >= key, s, -jnp.inf)
            m_new = jnp.maximum(m, jnp.max(s, axis=0, keepdims=True))
            p = jnp.exp2(s - m_new).astype(BF16)
            vt = vt_ref[0, j, hh * VT_ROWS:(hh + 1) * VT_ROWS, :]
            acc = jnp.exp2(m - m_new) * acc + _dot(vt, p)
            out.append((m_new, acc))
        return tuple(out)

    init = tuple((jnp.full((1, tile), -jnp.inf, F32), jnp.zeros((VT_ROWS, tile), F32)) for _ in range(nh))
    carry = lax.fori_loop(0, i, lambda j, c: step(j, c, False), init)
    carry = step(i, carry, True)
    out_t = jnp.concatenate([acc[:dh] / acc[dh:dh + 1] for _, acc in carry], axis=0)
    o_ref[0] = out_t.T.astype(BF16)


def _attn_call(qp, kp, vt):
    b, s, _ = qp.shape
    tile = min(ATTN_TILE, s)
    nh = ATTN_HEADS_PER_STEP
    return pl.pallas_call(
        functools.partial(_attn_body, tile=tile),
        grid=(b, FOX_HEADS // nh, s // tile),
        in_specs=[pl.BlockSpec((1, tile, nh * LANES), lambda bi, hi, qi: (bi, qi, hi)),
                  pl.BlockSpec((1, s, nh * LANES), lambda bi, hi, qi: (bi, 0, hi)),
                  pl.BlockSpec((1, s // tile, nh * VT_ROWS, tile), lambda bi, hi, qi: (bi, 0, hi, 0))],
        out_specs=pl.BlockSpec((1, tile, nh * FOX_HEAD_DIM), lambda bi, hi, qi: (bi, qi, hi)),
        out_shape=jax.ShapeDtypeStruct((b, s, FOX_WIDTH), BF16),
        compiler_params=pltpu.CompilerParams(dimension_semantics=("arbitrary", "arbitrary", "arbitrary"),
                                             vmem_limit_bytes=VMEM_LIMIT_BYTES),
        name="fox_attention",
    )(qp, kp, vt)


def _gdn_body(q_ref, k_ref, v_ref, sgg_ref, small_ref, smallt_ref, gain_ref, y_ref, state_ref, *, n_chunks):
    @pl.when(pl.program_id(1) == 0)
    def _():
        state_ref[...] = jnp.zeros_like(state_ref)

    n = GDN_CHUNK
    row = lax.broadcasted_iota(jnp.int32, (n, n), 0)
    col = lax.broadcasted_iota(jnp.int32, (n, n), 1)
    incl = row >= col
    strict = row > col
    eye = (row == col).astype(F32)

    pairs = [(c, h) for c in range(n_chunks) for h in range(GDN_HEADS)]
    pre = []
    for c, h in pairs:
        rows = slice(c * n, (c + 1) * n)
        hs = slice(h * GDN_HEAD_DIM, (h + 1) * GDN_HEAD_DIM)
        sm = small_ref[0, rows, :]
        smt = smallt_ref[0, :, rows]
        gcc = jnp.sum(jnp.where(col == LANE_GC + h, sm, 0.0), axis=1, keepdims=True)
        beta = jnp.sum(jnp.where(col == LANE_BETA + h, sm, 0.0), axis=1, keepdims=True)
        gcr = jnp.sum(jnp.where(row[:GATE_ROWS] == LANE_GC + h, smt, 0.0), axis=0, keepdims=True)
        g_last = jnp.sum(jnp.where(col[0:1, :] == n - 1, gcr, 0.0), axis=1, keepdims=True)
        k_b = k_ref[0, rows, hs]
        k = k_b.astype(F32)
        kb = k * beta
        pre.append(dict(rows=rows, hs=hs, gcc=gcc, g_last=g_last, k_b=k_b, k=k, kb=kb,
                        q_b=q_ref[0, rows, hs], vb=v_ref[0, rows, hs].astype(F32) * beta,
                        decay=jnp.exp(jnp.where(incl, gcc - gcr, -jnp.inf)), e_g=jnp.exp(gcc)))

    x_pow = [jnp.where(strict, -(_dot_nt(p["kb"].astype(BF16), p["k_b"]) * p["decay"]), 0.0) for p in pre]
    t = [eye + x for x in x_pow]
    x_pow = [_dot(x.astype(BF16), x.astype(BF16)) for x in x_pow]
    for step in range(1, NEUMANN_STEPS + 1):
        if step < NEUMANN_STEPS:
            both = [_dot(jnp.concatenate([x, ti], axis=0).astype(BF16), x.astype(BF16)) for x, ti in zip(x_pow, t)]
            x_pow = [r[:n] for r in both]
            t = [ti + r[n:] for ti, r in zip(t, both)]
        else:
            t = [ti + _dot(ti.astype(BF16), x.astype(BF16)) for x, ti in zip(x_pow, t)]

    uw = [_dot(ti.astype(BF16), jnp.concatenate([p["vb"], p["kb"] * p["e_g"]], axis=1).astype(BF16))
          for ti, p in zip(t, pre)]
    attn = [jnp.where(incl, _dot_nt(p["q_b"], p["k_b"]) * p["decay"], 0.0).astype(BF16) for p in pre]

    states = [state_ref[h] for h in range(GDN_HEADS)]
    for c in range(n_chunks):
        idx = [c * GDN_HEADS + h for h in range(GDN_HEADS)]
        st_b = [st.astype(BF16) for st in states]
        v_new = [(uw[i][:, :GDN_HEAD_DIM] - _dot(uw[i][:, GDN_HEAD_DIM:].astype(BF16), st_b[h])).astype(BF16)
                 for h, i in enumerate(idx)]
        new_states = []
        for h, i in enumerate(idx):
            p = pre[i]
            kd_t = (p["k"] * jnp.exp(p["g_last"] - p["gcc"])).T.astype(BF16)
            new_states.append(states[h] * jnp.exp(p["g_last"]) + _dot(kd_t, v_new[h]))
        for h, i in enumerate(idx):
            p = pre[i]
            qd = (p["q_b"].astype(F32) * p["e_g"]).astype(BF16)
            o = _dot(jnp.concatenate([qd, attn[i]], axis=1), jnp.concatenate([st_b[h], v_new[h]], axis=0))
            y = _rms_norm(o, gain_ref[...]) * sgg_ref[0, p["rows"], p["hs"]].astype(F32)
            y_ref[0, p["rows"], p["hs"]] = y.astype(BF16)
        states = new_states

    for h in range(GDN_HEADS):
        state_ref[h] = states[h]


def _gdn_call(gq, gk, gv, sgg, small, small_t, out_gain):
    b, s, _ = gq.shape
    n_chunks = min(GDN_CHUNKS_PER_STEP, s // GDN_CHUNK)
    ts = n_chunks * GDN_CHUNK
    tok = pl.BlockSpec((1, ts, GDN_WIDTH), lambda bi, ci: (bi, ci, 0))
    return pl.pallas_call(
        functools.partial(_gdn_body, n_chunks=n_chunks),
        grid=(b, s // ts),
        in_specs=[tok, tok, tok, tok,
                  pl.BlockSpec((1, ts, LANES), lambda bi, ci: (bi, ci, 0)),
                  pl.BlockSpec((1, GATE_ROWS, ts), lambda bi, ci: (bi, 0, ci)),
                  _const_spec((1, GDN_HEAD_DIM))],
        out_specs=tok,
        out_shape=jax.ShapeDtypeStruct((b, s, GDN_WIDTH), BF16),
        scratch_shapes=[pltpu.VMEM((GDN_HEADS, GDN_HEAD_DIM, GDN_HEAD_DIM), F32)],
        compiler_params=pltpu.CompilerParams(dimension_semantics=("arbitrary", "arbitrary"),
                                             vmem_limit_bytes=VMEM_LIMIT_BYTES),
        name="gated_delta_rule",
    )(gq, gk, gv, sgg, small, small_t, out_gain.reshape(1, GDN_HEAD_DIM).astype(F32))


def kernel(x, ffn1_norm, ffn1_w_in, ffn1_w_out, mix_norm, w_in, fox_q_norm, fox_k_norm, fox_f_bias,
           gdn_conv, gdn_a_log, gdn_dt_bias, gdn_out_norm, w_out, ffn2_norm, ffn2_w_in, ffn2_w_out):
    b, s, d = x.shape
    depth = ffn1_norm.shape[0]
    h = x.reshape(b * s, d)
    for l in range(depth):
        h = _ffn_call(h, ffn1_norm[l], ffn1_w_in[l], ffn1_w_out[l])
        qp, kp, vt, gq, gk, gv, sgg, small, small_t = _proj_call(
            h.reshape(b, s, d), mix_norm[l], w_in[l], fox_q_norm[l], fox_k_norm[l], fox_f_bias[l],
            gdn_conv[l], gdn_a_log[l], gdn_dt_bias[l])
        y_fox = _attn_call(qp, kp, vt)
        y_gdn = _gdn_call(gq, gk, gv, sgg, small, small_t, gdn_out_norm[l])
        wo = w_out[l].astype(BF16)
        h = _ffn_call(h, ffn2_norm[l], ffn2_w_in[l], ffn2_w_out[l],
                      mix=(y_fox.reshape(b * s, FOX_WIDTH), y_gdn.reshape(b * s, GDN_WIDTH),
                           wo[:FOX_WIDTH], wo[FOX_WIDTH:]))
    return h.reshape(b, s, d)
```

```python
import functools

import jax
import jax.numpy as jnp
from jax import lax
from jax.experimental import pallas as pl
from jax.experimental.pallas import tpu as pltpu

F32 = jnp.float32
BF16 = jnp.bfloat16

D_MODEL = 1024
FOX_HEADS = 8
FOX_HEAD_DIM = 64
FOX_WIDTH = FOX_HEADS * FOX_HEAD_DIM
GDN_HEADS = 4
GDN_HEAD_DIM = 128
GDN_WIDTH = GDN_HEADS * GDN_HEAD_DIM
CONV_WIDTH = 4
D_FF = 2816
EPS = 1e-6

LANES = 128
SUBLANES = 8
VMEM_LIMIT_BYTES = 56 * 1024 * 1024

FOX_PAD = FOX_HEADS * LANES
OFF_Q = 0
OFF_K = OFF_Q + FOX_PAD
OFF_V = OFF_K + FOX_PAD
OFF_G = OFF_V + FOX_WIDTH
OFF_GG = OFF_G + 3 * GDN_WIDTH
OFF_SMALL = OFF_GG + GDN_WIDTH
N_PROJ = OFF_SMALL + LANES

LANE_CUM = 0
LANE_GC = FOX_HEADS
LANE_BETA = LANE_GC + GDN_HEADS
GATE_ROWS = 16
PIECE_LANES = GATE_ROWS

BIAS_LANE = FOX_HEAD_DIM
LOG2E = 1.4426950408889634
BF16_SUBLANES = 16
VT_ROWS = FOX_HEAD_DIM + BF16_SUBLANES

GDN_CHUNK = LANES
NEUMANN_STEPS = 6
GDN_CHUNKS_PER_STEP = 4

FFN_SUBTILE_M = 512
FFN_TILE_M = 2 * FFN_SUBTILE_M
FFN_TILE_F = 256
ATTN_TILE = 512
PROJ_TILE_M = ATTN_TILE
ATTN_HEADS_PER_STEP = 8


def _dot(a, b):
    return jnp.dot(a, b, preferred_element_type=F32)


def _dot_nt(a, b):
    return lax.dot_general(a, b, (((1,), (1,)), ((), ())), preferred_element_type=F32)


def _keep_bf16_bits(v):
    bits = lax.bitcast_convert_type(v, jnp.uint32) & jnp.uint32(0xFFFF0000)
    return lax.bitcast_convert_type(bits, F32)


def _split3(v):
    hi = _keep_bf16_bits(v)
    r1 = v - hi
    mid = _keep_bf16_bits(r1)
    lo = _keep_bf16_bits(r1 - mid)
    return hi, mid, lo


def _pack3(v, lane, one_lane=False):
    hi, mid, lo = _split3(v)
    tail = jnp.where(lane == 3 * PIECE_LANES, 1.0, 0.0) if one_lane else 0.0
    packed = jnp.where(lane < PIECE_LANES, hi,
                       jnp.where(lane < 2 * PIECE_LANES, pltpu.roll(mid, PIECE_LANES, 1),
                                 jnp.where(lane < 3 * PIECE_LANES, pltpu.roll(lo, 2 * PIECE_LANES, 1), tail)))
    return packed.astype(BF16)


def _rms_norm(x, gain_row):
    ms = jnp.mean(x * x, axis=-1, keepdims=True)
    return x * lax.rsqrt(ms + EPS) * gain_row


def _silu(x):
    return x / (1.0 + jnp.exp(-x))


def _const_spec(shape):
    nd = len(shape)
    return pl.BlockSpec(shape, lambda *_: (0,) * nd, pipeline_mode=pl.Buffered(1))


def _ffn_body(*refs, n_chunks, tile_f, has_mix):
    if has_mix:
        x_ref, yf_ref, yg_ref, wof_ref, wog_ref, g_ref, win_ref, wout_ref, o_ref = refs
    else:
        x_ref, g_ref, win_ref, wout_ref, gnext_ref, o_ref, xn_next_ref = refs
    for r in range(x_ref.shape[0] // FFN_SUBTILE_M):
        rows = slice(r * FFN_SUBTILE_M, (r + 1) * FFN_SUBTILE_M)
        x = x_ref[rows, :]
        if has_mix:
            x = x + _dot(yf_ref[rows, :], wof_ref[...]) + _dot(yg_ref[rows, :], wog_ref[...])
        xn = _rms_norm(x, g_ref[...]).astype(BF16)
        acc = None
        for c in range(n_chunks):
            cols = slice(c * tile_f, (c + 1) * tile_f)
            gate = _dot(xn, win_ref[:, cols])
            up = _dot(xn, win_ref[:, D_FF + c * tile_f:D_FF + (c + 1) * tile_f])
            act = (_silu(gate) * up).astype(BF16)
            part = _dot(act, wout_ref[cols, :])
            acc = part if acc is None else acc + part
        out = x + 0.5 * acc
        o_ref[rows, :] = out
        if not has_mix:
            xn_next_ref[rows, :] = _rms_norm(out, gnext_ref[...]).astype(BF16)


def _ffn_call(x2d, gain, w_in, w_out, mix=None, next_gain=None):
    m, d = x2d.shape
    tm = min(FFN_TILE_M, m)
    tf = FFN_TILE_F
    nc = D_FF // tf
    w_in_r = w_in.astype(BF16)
    w_out_r = w_out.astype(BF16)
    row = pl.BlockSpec((tm, d), lambda i: (i, 0))
    in_specs = [row]
    args = [x2d]
    if mix is not None:
        y_fox, y_gdn, wo_f, wo_g = mix
        in_specs += [pl.BlockSpec((tm, FOX_WIDTH), lambda i: (i, 0)),
                     pl.BlockSpec((tm, GDN_WIDTH), lambda i: (i, 0)),
                     _const_spec(wo_f.shape), _const_spec(wo_g.shape)]
        args += [y_fox, y_gdn, wo_f, wo_g]
    in_specs += [_const_spec((1, d)), _const_spec(w_in_r.shape), _const_spec(w_out_r.shape)]
    args += [gain.reshape(1, d), w_in_r, w_out_r]
    out_specs = row
    out_shape = jax.ShapeDtypeStruct((m, d), F32)
    if mix is None:
        in_specs += [_const_spec((1, d))]
        args += [next_gain.reshape(1, d)]
        out_specs = [row, row]
        out_shape = [out_shape, jax.ShapeDtypeStruct((m, d), BF16)]
    return pl.pallas_call(
        functools.partial(_ffn_body, n_chunks=nc, tile_f=tf, has_mix=mix is not None),
        grid=(m // tm,),
        in_specs=in_specs,
        out_specs=out_specs,
        out_shape=out_shape,
        compiler_params=pltpu.CompilerParams(dimension_semantics=("arbitrary",),
                                             vmem_limit_bytes=VMEM_LIMIT_BYTES),
        name="ffn_mix" if mix is not None else "ffn",
    )(*args)


def _proj_body(xn_ref, w_ref, cw_ref, qg_ref, browt_ref, alogt_ref, triu_ref, pmat_ref,
               qp_ref, kp_ref, vt_ref, gq_ref, gk_ref, gv_ref, sgg_ref, small_ref, smallt_ref,
               cum_carry, conv_buf, pack_buf, *, tm):
    s = pl.program_id(1)

    @pl.when(s == 0)
    def _():
        cum_carry[...] = jnp.zeros_like(cum_carry)
        conv_buf[0:SUBLANES, :] = jnp.zeros((SUBLANES, 3 * GDN_WIDTH), F32)

    xn = xn_ref[0]
    lane = lax.broadcasted_iota(jnp.int32, (tm, LANES), 1)

    z = _dot(xn, w_ref[:, OFF_SMALL:OFF_SMALL + LANES])
    pg = _dot(xn, w_ref[:, OFF_G:OFF_G + 3 * GDN_WIDTH])

    zt = z.T[0:GATE_ROWS] + browt_ref[...]
    grow = lax.broadcasted_iota(jnp.int32, (GATE_ROWS, tm), 0)
    e = jnp.log1p(jnp.exp(-jnp.abs(zt)))
    log_sig = jnp.minimum(zt, 0.0) - e
    softplus = jnp.maximum(zt, 0.0) + e
    sigmoid = 1.0 / (1.0 + jnp.exp(-zt))
    log_decay = -jnp.exp(alogt_ref[...]) * softplus
    pieces = jnp.concatenate(_split3(jnp.where(grow < LANE_GC, log_sig, log_decay)), axis=0).astype(BF16)
    c3 = _dot(pieces, triu_ref[...])
    c = c3[0:GATE_ROWS] + c3[GATE_ROWS:2 * GATE_ROWS] + c3[2 * GATE_ROWS:]
    cum_t = c[:, :tm] + jnp.concatenate([cum_carry[...]] * (tm // LANES), axis=1)
    pq = _dot(xn, w_ref[:, OFF_Q:OFF_Q + FOX_PAD])
    cum_carry[...] = jnp.broadcast_to(cum_t[:, tm - 1:tm], (GATE_ROWS, LANES))
    small_t = jnp.where(grow < LANE_GC, cum_t, jnp.where(grow < LANE_BETA, c[:, tm:], sigmoid))
    smallt_ref[0] = small_t
    small = jnp.concatenate([small_t, jnp.zeros((LANES - GATE_ROWS, tm), F32)], axis=0).T
    small_ref[0] = small

    pack_buf[...] = _pack3(small * LOG2E, lane, one_lane=True)
    bias = _dot(pack_buf[...], pmat_ref[...])
    pk = _dot(xn, w_ref[:, OFF_K:OFF_K + FOX_PAD])
    pv = _dot(xn, w_ref[:, OFF_V:OFF_V + FOX_WIDTH])
    pgg = _dot(xn, w_ref[:, OFF_GG:OFF_GG + GDN_WIDTH])
    for off, p, out_ref in ((OFF_Q, pq, qp_ref), (OFF_K, pk, kp_ref)):
        for h in range(FOX_HEADS):
            blk = p[:, h * LANES:(h + 1) * LANES]
            ms = jnp.sum(blk * blk, axis=-1, keepdims=True) * (1.0 / FOX_HEAD_DIM)
            nrm = blk * lax.rsqrt(ms + EPS)
            if off == OFF_Q:
                nrm = nrm * qg_ref[...]
            out_ref[0, :, h * LANES:(h + 1) * LANES] = (
                nrm + bias[:, off + h * LANES:off + (h + 1) * LANES]).astype(BF16)

    v_t = pv.T.astype(BF16)
    for h in range(FOX_HEADS):
        vt_ref[0, 0, h * VT_ROWS:h * VT_ROWS + FOX_HEAD_DIM, :] = v_t[h * FOX_HEAD_DIM:(h + 1) * FOX_HEAD_DIM, :]
        vt_ref[0, 0, h * VT_ROWS + FOX_HEAD_DIM:(h + 1) * VT_ROWS, :] = jnp.ones((VT_ROWS - FOX_HEAD_DIM, tm), BF16)

    conv_buf[SUBLANES:SUBLANES + tm, :] = pg
    ext = conv_buf[...]
    y = cw_ref[CONV_WIDTH - 1:CONV_WIDTH, :] * pg
    for back in range(1, CONV_WIDTH):
        k = CONV_WIDTH - 1 - back
        y = y + cw_ref[k:k + 1, :] * pltpu.roll(ext, back, 0)[SUBLANES:SUBLANES + tm, :]
    conv_buf[0:SUBLANES, :] = conv_buf[tm:tm + SUBLANES, :]
    y = _silu(y)
    for idx, out_ref in enumerate((gq_ref, gk_ref, gv_ref)):
        for h in range(GDN_HEADS):
            lo = idx * GDN_WIDTH + h * GDN_HEAD_DIM
            blk = y[:, lo:lo + GDN_HEAD_DIM]
            if idx < 2:
                blk = blk * lax.rsqrt(jnp.sum(blk * blk, axis=-1, keepdims=True) + EPS)
            if idx == 0:
                blk = blk * (GDN_HEAD_DIM ** -0.5)
            out_ref[0, :, h * GDN_HEAD_DIM:(h + 1) * GDN_HEAD_DIM] = blk.astype(BF16)

    sgg_ref[0] = _silu(pgg).astype(BF16)


def _proj_weights(w_in, fox_q_norm, fox_k_norm, fox_f_bias, gdn_a_log, gdn_dt_bias):
    d = w_in.shape[0]
    fw = FOX_WIDTH
    fq, fk, fv = w_in[:, 0:fw], w_in[:, fw:2 * fw], w_in[:, 2 * fw:3 * fw]
    o = 3 * fw
    ff = w_in[:, o:o + FOX_HEADS]
    o += FOX_HEADS
    gqkv = w_in[:, o:o + 3 * GDN_WIDTH]
    o += 3 * GDN_WIDTH
    ga = w_in[:, o:o + GDN_HEADS]
    gb = w_in[:, o + GDN_HEADS:o + 2 * GDN_HEADS]
    gg = w_in[:, o + 2 * GDN_HEADS:]

    def pad_heads(w):
        w = w.reshape(d, FOX_HEADS, FOX_HEAD_DIM)
        return jnp.pad(w, ((0, 0), (0, 0), (0, LANES - FOX_HEAD_DIM))).reshape(d, FOX_PAD)

    small = jnp.pad(jnp.concatenate([ff, ga, gb], axis=1), ((0, 0), (0, LANES - FOX_HEADS - 2 * GDN_HEADS)))
    w = jnp.concatenate([pad_heads(fq), pad_heads(fk), fv, gqkv, gg, small], axis=1).astype(BF16)

    def lane_row(vals, start):
        return jnp.zeros((1, LANES), F32).at[0, start:start + vals.shape[0]].set(vals.astype(F32))

    qg = lane_row(fox_q_norm * fox_k_norm * (FOX_HEAD_DIM ** -0.5 * LOG2E), 0)
    brow = lane_row(fox_f_bias, LANE_CUM) + lane_row(gdn_dt_bias, LANE_GC)
    alog = lane_row(gdn_a_log, LANE_GC)

    pm = jnp.zeros((LANES, 2, FOX_HEADS, LANES), F32)
    hh = jnp.arange(FOX_HEADS)
    for piece in range(3):
        src = piece * PIECE_LANES + LANE_CUM + hh
        pm = pm.at[src, 0, hh, BIAS_LANE + piece].set(1.0)
        pm = pm.at[src, 1, hh, BIAS_LANE + 3 + piece].set(-1.0)
        pm = pm.at[3 * PIECE_LANES, 0, hh, BIAS_LANE + 3 + piece].set(1.0)
        pm = pm.at[3 * PIECE_LANES, 1, hh, BIAS_LANE + piece].set(1.0)
    pmat = pm.reshape(LANES, 2 * FOX_PAD).astype(BF16)
    return w, qg, brow, alog, pmat


def _proj_call(x3d, w_in, fox_q_norm, fox_k_norm, fox_f_bias, gdn_conv, gdn_a_log, gdn_dt_bias):
    b, s, d = x3d.shape
    tm = min(PROJ_TILE_M, s)
    w, qg, brow, alog, pmat = _proj_weights(w_in, fox_q_norm, fox_k_norm, fox_f_bias, gdn_a_log, gdn_dt_bias)
    r = jnp.arange(tm)
    upper = r[:, None] <= r[None, :]
    same_chunk = (r[:, None] // GDN_CHUNK) == (r[None, :] // GDN_CHUNK)
    triu = jnp.concatenate([upper, upper & same_chunk], axis=1).astype(BF16)
    brow_t = jnp.broadcast_to(brow[0, :GATE_ROWS, None], (GATE_ROWS, tm))
    alog_t = jnp.broadcast_to(alog[0, :GATE_ROWS, None], (GATE_ROWS, tm))

    def tok(width, dtype):
        return (pl.BlockSpec((1, tm, width), lambda bi, si: (bi, si, 0)),
                jax.ShapeDtypeStruct((b, s, width), dtype))

    outs = [tok(FOX_PAD, BF16), tok(FOX_PAD, BF16),
            (pl.BlockSpec((1, 1, FOX_HEADS * VT_ROWS, tm), lambda bi, si: (bi, si, 0, 0)),
             jax.ShapeDtypeStruct((b, s // tm, FOX_HEADS * VT_ROWS, tm), BF16)),
            tok(GDN_WIDTH, BF16), tok(GDN_WIDTH, BF16), tok(GDN_WIDTH, BF16),
            tok(GDN_WIDTH, BF16), tok(LANES, F32),
            (pl.BlockSpec((1, GATE_ROWS, tm), lambda bi, si: (bi, 0, si)),
             jax.ShapeDtypeStruct((b, GATE_ROWS, s), F32))]
    consts = [w, gdn_conv.astype(F32), qg, brow_t, alog_t, triu, pmat]
    return pl.pallas_call(
        functools.partial(_proj_body, tm=tm),
        grid=(b, s // tm),
        in_specs=[pl.BlockSpec((1, tm, d), lambda bi, si: (bi, si, 0))] + [_const_spec(c.shape) for c in consts],
        out_specs=[o[0] for o in outs],
        out_shape=[o[1] for o in outs],
        scratch_shapes=[pltpu.VMEM((GATE_ROWS, LANES), F32),
                        pltpu.VMEM((tm + SUBLANES, 3 * GDN_WIDTH), F32),
                        pltpu.VMEM((tm, LANES), BF16)],
        compiler_params=pltpu.CompilerParams(dimension_semantics=("arbitrary", "arbitrary"),
                                             vmem_limit_bytes=VMEM_LIMIT_BYTES),
        name="mixer_proj",
    )(x3d, *consts)


def _attn_body(q_ref, k_ref, vt_ref, o_ref, *, tile):
    i = pl.program_id(2)
    nh = ATTN_HEADS_PER_STEP
    dh = FOX_HEAD_DIM

    def logits(hh, j):
        off = pl.multiple_of(j * tile, tile)
        q = q_ref[0, :, hh * LANES:(hh + 1) * LANES]
        k = k_ref[0, pl.ds(off, tile), hh * LANES:(hh + 1) * LANES]
        return _dot_nt(k, q)

    def step(j, carry, diagonal):
        ahead = 2
        s_all = {hh: logits(hh, j) for hh in range(min(ahead, nh))}
        out = []
        for hh in range(nh):
            if hh + ahead < nh:
                s_all[hh + ahead] = logits(hh + ahead, j)
            m, acc = carry[hh]
            s = s_all.pop(hh)
            if diagonal:
                key = lax.broadcasted_iota(jnp.int32, s.shape, 0)
                qry = lax.broadcasted_iota(jnp.int32, s.shape, 1)
                s = jnp.where(qry >= key, s, -jnp.inf)
            m_new = jnp.maximum(m, jnp.max(s, axis=0, keepdims=True))
            p = jnp.exp2(s - m_new).astype(BF16)
            vt = vt_ref[0, j, hh * VT_ROWS:(hh + 1) * VT_ROWS, :]
            acc = jnp.exp2(m - m_new) * acc + _dot(vt, p)
            out.append((m_new, acc))
        return tuple(out)

    init = tuple((jnp.full((1, tile), -jnp.inf, F32), jnp.zeros((VT_ROWS, tile), F32)) for _ in range(nh))
    carry = lax.fori_loop(0, i, lambda j, c: step(j, c, False), init)
    carry = step(i, carry, True)
    out_t = jnp.concatenate([acc[:dh] / acc[dh:dh + 1] for _, acc in carry], axis=0)
    o_ref[0] = out_t.T.astype(BF16)


def _attn_call(qp, kp, vt):
    b, s, _ = qp.shape
    tile = min(ATTN_TILE, s)
    nh = ATTN_HEADS_PER_STEP
    return pl.pallas_call(
        functools.partial(_attn_body, tile=tile),
        grid=(b, FOX_HEADS // nh, s // tile),
        in_specs=[pl.BlockSpec((1, tile, nh * LANES), lambda bi, hi, qi: (bi, qi, hi)),
                  pl.BlockSpec((1, s, nh * LANES), lambda bi, hi, qi: (bi, 0, hi)),
                  pl.BlockSpec((1, s // tile, nh * VT_ROWS, tile), lambda bi, hi, qi: (bi, 0, hi, 0))],
        out_specs=pl.BlockSpec((1, tile, nh * FOX_HEAD_DIM), lambda bi, hi, qi: (bi, qi, hi)),
        out_shape=jax.ShapeDtypeStruct((b, s, FOX_WIDTH), BF16),
        compiler_params=pltpu.CompilerParams(dimension_semantics=("arbitrary", "arbitrary", "arbitrary"),
                                             vmem_limit_bytes=VMEM_LIMIT_BYTES),
        name="fox_attention",
    )(qp, kp, vt)


def _gdn_body(q_ref, k_ref, v_ref, sgg_ref, small_ref, smallt_ref, gain_ref, y_ref, state_ref, *, n_chunks):
    @pl.when(pl.program_id(1) == 0)
    def _():
        state_ref[...] = jnp.zeros_like(state_ref)

    n = GDN_CHUNK
    row = lax.broadcasted_iota(jnp.int32, (n, n), 0)
    col = lax.broadcasted_iota(jnp.int32, (n, n), 1)
    incl = row >= col
    strict = row > col
    eye = (row == col).astype(F32)

    pairs = [(c, h) for c in range(n_chunks) for h in range(GDN_HEADS)]
    pre = []
    for c, h in pairs:
        rows = slice(c * n, (c + 1) * n)
        hs = slice(h * GDN_HEAD_DIM, (h + 1) * GDN_HEAD_DIM)
        sm = small_ref[0, rows, :]
        smt = smallt_ref[0, :, rows]
        gcc = jnp.sum(jnp.where(col == LANE_GC + h, sm, 0.0), axis=1, keepdims=True)
        beta = jnp.sum(jnp.where(col == LANE_BETA + h, sm, 0.0), axis=1, keepdims=True)
        gcr = jnp.sum(jnp.where(row[:GATE_ROWS] == LANE_GC + h, smt, 0.0), axis=0, keepdims=True)
        g_last = jnp.sum(jnp.where(col[0:1, :] == n - 1, gcr, 0.0), axis=1, keepdims=True)
        k_b = k_ref[0, rows, hs]
        k = k_b.astype(F32)
        kb = k * beta
        pre.append(dict(rows=rows, hs=hs, gcc=gcc, g_last=g_last, k_b=k_b, k=k, kb=kb,
                        q_b=q_ref[0, rows, hs], vb=v_ref[0, rows, hs].astype(F32) * beta,
                        decay=jnp.exp(jnp.where(incl, gcc - gcr, -jnp.inf)), e_g=jnp.exp(gcc)))

    x_pow = [jnp.where(strict, -(_dot_nt(p["kb"].astype(BF16), p["k_b"]) * p["decay"]), 0.0) for p in pre]
    t = [eye + x for x in x_pow]
    x_pow = [_dot(x.astype(BF16), x.astype(BF16)) for x in x_pow]
    for step in range(1, NEUMANN_STEPS + 1):
        if step < NEUMANN_STEPS:
            both = [_dot(jnp.concatenate([x, ti], axis=0).astype(BF16), x.astype(BF16)) for x, ti in zip(x_pow, t)]
            x_pow = [r[:n] for r in both]
            t = [ti + r[n:] for ti, r in zip(t, both)]
        else:
            t = [ti + _dot(ti.astype(BF16), x.astype(BF16)) for x, ti in zip(x_pow, t)]

    uw = [_dot(ti.astype(BF16), jnp.concatenate([p["vb"], p["kb"] * p["e_g"]], axis=1).astype(BF16))
          for ti, p in zip(t, pre)]
    attn = [jnp.where(incl, _dot_nt(p["q_b"], p["k_b"]) * p["decay"], 0.0).astype(BF16) for p in pre]

    states = [state_ref[h] for h in range(GDN_HEADS)]
    for c in range(n_chunks):
        idx = [c * GDN_HEADS + h for h in range(GDN_HEADS)]
        st_b = [st.astype(BF16) for st in states]
        v_new = [(uw[i][:, :GDN_HEAD_DIM] - _dot(uw[i][:, GDN_HEAD_DIM:].astype(BF16), st_b[h])).astype(BF16)
                 for h, i in enumerate(idx)]
        new_states = []
        for h, i in enumerate(idx):
            p = pre[i]
            kd_t = (p["k"] * jnp.exp(p["g_last"] - p["gcc"])).T.astype(BF16)
            new_states.append(states[h] * jnp.exp(p["g_last"]) + _dot(kd_t, v_new[h]))
        for h, i in enumerate(idx):
            p = pre[i]
            qd = (p["q_b"].astype(F32) * p["e_g"]).astype(BF16)
            o = _dot(jnp.concatenate([qd, attn[i]], axis=1), jnp.concatenate([st_b[h], v_new[h]], axis=0))
            y = _rms_norm(o, gain_ref[...]) * sgg_ref[0, p["rows"], p["hs"]].astype(F32)
            y_ref[0, p["rows"], p["hs"]] = y.astype(BF16)
        states = new_states

    for h in range(GDN_HEADS):
        state_ref[h] = states[h]


def _gdn_call(gq, gk, gv, sgg, small, small_t, out_gain):
    b, s, _ = gq.shape
    n_chunks = min(GDN_CHUNKS_PER_STEP, s // GDN_CHUNK)
    ts = n_chunks * GDN_CHUNK
    tok = pl.BlockSpec((1, ts, GDN_WIDTH), lambda bi, ci: (bi, ci, 0))
    return pl.pallas_call(
        functools.partial(_gdn_body, n_chunks=n_chunks),
        grid=(b, s // ts),
        in_specs=[tok, tok, tok, tok,
                  pl.BlockSpec((1, ts, LANES), lambda bi, ci: (bi, ci, 0)),
                  pl.BlockSpec((1, GATE_ROWS, ts), lambda bi, ci: (bi, 0, ci)),
                  _const_spec((1, GDN_HEAD_DIM))],
        out_specs=tok,
        out_shape=jax.ShapeDtypeStruct((b, s, GDN_WIDTH), BF16),
        scratch_shapes=[pltpu.VMEM((GDN_HEADS, GDN_HEAD_DIM, GDN_HEAD_DIM), F32)],
        compiler_params=pltpu.CompilerParams(dimension_semantics=("arbitrary", "arbitrary"),
                                             vmem_limit_bytes=VMEM_LIMIT_BYTES),
        name="gated_delta_rule",
    )(gq, gk, gv, sgg, small, small_t, out_gain.reshape(1, GDN_HEAD_DIM).astype(F32))


def kernel(x, ffn1_norm, ffn1_w_in, ffn1_w_out, mix_norm, w_in, fox_q_norm, fox_k_norm, fox_f_bias,
           gdn_conv, gdn_a_log, gdn_dt_bias, gdn_out_norm, w_out, ffn2_norm, ffn2_w_in, ffn2_w_out):
    b, s, d = x.shape
    depth = ffn1_norm.shape[0]
    h = x.reshape(b * s, d)
    for l in range(depth):
        h, hn = _ffn_call(h, ffn1_norm[l], ffn1_w_in[l], ffn1_w_out[l], next_gain=mix_norm[l])
        qp, kp, vt, gq, gk, gv, sgg, small, small_t = _proj_call(
            hn.reshape(b, s, d), w_in[l], fox_q_norm[l], fox_k_norm[l], fox_f_bias[l],
            gdn_conv[l], gdn_a_log[l], gdn_dt_bias[l])
        y_fox = _attn_call(qp, kp, vt)
        y_gdn = _gdn_call(gq, gk, gv, sgg, small, small_t, gdn_out_norm[l])
        wo = w_out[l].astype(BF16)
        h = _ffn_call(h, ffn2_norm[l], ffn2_w_in[l], ffn2_w_out[l],
                      mix=(y_fox.reshape(b * s, FOX_WIDTH), y_gdn.reshape(b * s, GDN_WIDTH),
                           wo[:FOX_WIDTH], wo[FOX_WIDTH:]))
    return h.reshape(b, s, d)
```

```python
import functools

import jax
import jax.numpy as jnp
from jax import lax
from jax.experimental import pallas as pl
from jax.experimental.pallas import tpu as pltpu

F32 = jnp.float32
BF16 = jnp.bfloat16

D_MODEL = 1024
FOX_HEADS = 8
FOX_HEAD_DIM = 64
FOX_WIDTH = FOX_HEADS * FOX_HEAD_DIM
GDN_HEADS = 4
GDN_HEAD_DIM = 128
GDN_WIDTH = GDN_HEADS * GDN_HEAD_DIM
CONV_WIDTH = 4
D_FF = 2816
EPS = 1e-6

LANES = 128
SUBLANES = 8
VMEM_LIMIT_BYTES = 56 * 1024 * 1024

FOX_PAD = FOX_HEADS * LANES
OFF_Q = 0
OFF_K = OFF_Q + FOX_PAD
OFF_V = OFF_K + FOX_PAD
OFF_G = OFF_V + FOX_WIDTH
OFF_GG = OFF_G + 3 * GDN_WIDTH
OFF_SMALL = OFF_GG + GDN_WIDTH
N_PROJ = OFF_SMALL + LANES

LANE_CUM = 0
LANE_GC = FOX_HEADS
LANE_BETA = LANE_GC + GDN_HEADS
GATE_ROWS = 16
PIECE_LANES = GATE_ROWS

BIAS_LANE = FOX_HEAD_DIM
LOG2E = 1.4426950408889634
BF16_SUBLANES = 16
VT_ROWS = FOX_HEAD_DIM + BF16_SUBLANES

GDN_CHUNK = LANES
NEUMANN_STEPS = 6
GDN_CHUNKS_PER_STEP = 8

FFN_SUBTILE_M = 512
FFN_TILE_M = 2 * FFN_SUBTILE_M
FFN_TILE_F = 256
ATTN_TILE = 512
PROJ_TILE_M = ATTN_TILE
PROJ_SUBTILES = 2
ATTN_HEADS_PER_STEP = 8


def _dot(a, b):
    return jnp.dot(a, b, preferred_element_type=F32)


def _dot_nt(a, b):
    return lax.dot_general(a, b, (((1,), (1,)), ((), ())), preferred_element_type=F32)


def _keep_bf16_bits(v):
    bits = lax.bitcast_convert_type(v, jnp.uint32) & jnp.uint32(0xFFFF0000)
    return lax.bitcast_convert_type(bits, F32)


def _split3(v):
    hi = _keep_bf16_bits(v)
    r1 = v - hi
    mid = _keep_bf16_bits(r1)
    lo = _keep_bf16_bits(r1 - mid)
    return hi, mid, lo


def _pack3(v, lane, one_lane=False):
    hi, mid, lo = _split3(v)
    tail = jnp.where(lane == 3 * PIECE_LANES, 1.0, 0.0) if one_lane else 0.0
    packed = jnp.where(lane < PIECE_LANES, hi,
                       jnp.where(lane < 2 * PIECE_LANES, pltpu.roll(mid, PIECE_LANES, 1),
                                 jnp.where(lane < 3 * PIECE_LANES, pltpu.roll(lo, 2 * PIECE_LANES, 1), tail)))
    return packed.astype(BF16)


def _rms_norm(x, gain_row):
    ms = jnp.mean(x * x, axis=-1, keepdims=True)
    return x * lax.rsqrt(ms + EPS) * gain_row


def _silu(x):
    return x / (1.0 + jnp.exp(-x))


def _const_spec(shape):
    nd = len(shape)
    return pl.BlockSpec(shape, lambda *_: (0,) * nd, pipeline_mode=pl.Buffered(1))


def _ffn_body(*refs, n_chunks, tile_f, has_mix):
    if has_mix:
        x_ref, yf_ref, yg_ref, wof_ref, wog_ref, g_ref, win_ref, wout_ref, o_ref = refs
    else:
        x_ref, g_ref, win_ref, wout_ref, gnext_ref, o_ref, xn_next_ref = refs
    for r in range(x_ref.shape[0] // FFN_SUBTILE_M):
        rows = slice(r * FFN_SUBTILE_M, (r + 1) * FFN_SUBTILE_M)
        x = x_ref[rows, :]
        if has_mix:
            x = x + _dot(yf_ref[rows, :], wof_ref[...]) + _dot(yg_ref[rows, :], wog_ref[...])
        xn = _rms_norm(x, g_ref[...]).astype(BF16)
        acc = None
        for c in range(n_chunks):
            cols = slice(c * tile_f, (c + 1) * tile_f)
            gate = _dot(xn, win_ref[:, cols])
            up = _dot(xn, win_ref[:, D_FF + c * tile_f:D_FF + (c + 1) * tile_f])
            act = (_silu(gate) * up).astype(BF16)
            part = _dot(act, wout_ref[cols, :])
            acc = part if acc is None else acc + part
        out = x + 0.5 * acc
        o_ref[rows, :] = out
        if not has_mix:
            xn_next_ref[rows, :] = _rms_norm(out, gnext_ref[...]).astype(BF16)


def _ffn_call(x2d, gain, w_in, w_out, mix=None, next_gain=None):
    m, d = x2d.shape
    tm = FFN_TILE_M if m % FFN_TILE_M == 0 else FFN_SUBTILE_M
    assert m % tm == 0, (m, tm)
    tf = FFN_TILE_F
    nc = D_FF // tf
    w_in_r = w_in.astype(BF16)
    w_out_r = w_out.astype(BF16)
    row = pl.BlockSpec((tm, d), lambda i: (i, 0))
    in_specs = [row]
    args = [x2d]
    if mix is not None:
        y_fox, y_gdn, wo_f, wo_g = mix
        in_specs += [pl.BlockSpec((tm, FOX_WIDTH), lambda i: (i, 0)),
                     pl.BlockSpec((tm, GDN_WIDTH), lambda i: (i, 0)),
                     _const_spec(wo_f.shape), _const_spec(wo_g.shape)]
        args += [y_fox, y_gdn, wo_f, wo_g]
    in_specs += [_const_spec((1, d)), _const_spec(w_in_r.shape), _const_spec(w_out_r.shape)]
    args += [gain.reshape(1, d), w_in_r, w_out_r]
    out_specs = row
    out_shape = jax.ShapeDtypeStruct((m, d), F32)
    if mix is None:
        in_specs += [_const_spec((1, d))]
        args += [next_gain.reshape(1, d)]
        out_specs = [row, row]
        out_shape = [out_shape, jax.ShapeDtypeStruct((m, d), BF16)]
    return pl.pallas_call(
        functools.partial(_ffn_body, n_chunks=nc, tile_f=tf, has_mix=mix is not None),
        grid=(m // tm,),
        in_specs=in_specs,
        out_specs=out_specs,
        out_shape=out_shape,
        compiler_params=pltpu.CompilerParams(dimension_semantics=("arbitrary",),
                                             vmem_limit_bytes=VMEM_LIMIT_BYTES),
        name="ffn_mix" if mix is not None else "ffn",
    )(*args)


def _proj_body(xn_ref, w_ref, cw_ref, qg_ref, browt_ref, alogt_ref, triu_ref, pmat_ref,
               qp_ref, kp_ref, vt_ref, gq_ref, gk_ref, gv_ref, sgg_ref, small_ref, smallt_ref,
               cum_carry, conv_a, conv_b, pack_buf, *, tm, n_sub):
    @pl.when(pl.program_id(1) == 0)
    def _():
        cum_carry[...] = jnp.zeros_like(cum_carry)
        conv_a[0:SUBLANES, :] = jnp.zeros((SUBLANES, 3 * GDN_WIDTH), F32)

    conv_bufs = (conv_a, conv_b)
    for sub in range(n_sub):
        _proj_tile(sub, slice(sub * tm, (sub + 1) * tm), conv_bufs[sub], conv_bufs[(sub + 1) % n_sub],
                   xn_ref, w_ref, cw_ref, qg_ref, browt_ref, alogt_ref, triu_ref, pmat_ref,
                   qp_ref, kp_ref, vt_ref, gq_ref, gk_ref, gv_ref, sgg_ref, small_ref, smallt_ref,
                   cum_carry, pack_buf, tm)


def _proj_tile(sub, rows, conv_buf, conv_next, xn_ref, w_ref, cw_ref, qg_ref, browt_ref, alogt_ref, triu_ref,
               pmat_ref, qp_ref, kp_ref, vt_ref, gq_ref, gk_ref, gv_ref, sgg_ref, small_ref, smallt_ref,
               cum_carry, pack_buf, tm):
    xn = xn_ref[0, rows, :]
    lane = lax.broadcasted_iota(jnp.int32, (tm, LANES), 1)

    z = _dot(xn, w_ref[:, OFF_SMALL:OFF_SMALL + LANES])
    pg = _dot(xn, w_ref[:, OFF_G:OFF_G + 3 * GDN_WIDTH])

    zt = z.T[0:GATE_ROWS] + browt_ref[...]
    grow = lax.broadcasted_iota(jnp.int32, (GATE_ROWS, tm), 0)
    e = jnp.log1p(jnp.exp(-jnp.abs(zt)))
    log_sig = jnp.minimum(zt, 0.0) - e
    softplus = jnp.maximum(zt, 0.0) + e
    sigmoid = 1.0 / (1.0 + jnp.exp(-zt))
    log_decay = -jnp.exp(alogt_ref[...]) * softplus
    pieces = jnp.concatenate(_split3(jnp.where(grow < LANE_GC, log_sig, log_decay)), axis=0).astype(BF16)
    c3 = _dot(pieces, triu_ref[...])
    c = c3[0:GATE_ROWS] + c3[GATE_ROWS:2 * GATE_ROWS] + c3[2 * GATE_ROWS:]
    cum_t = c[:, :tm] + jnp.concatenate([cum_carry[...]] * (tm // LANES), axis=1)
    pq = _dot(xn, w_ref[:, OFF_Q:OFF_Q + FOX_PAD])
    cum_carry[...] = jnp.broadcast_to(cum_t[:, tm - 1:tm], (GATE_ROWS, LANES))
    small_t = jnp.where(grow < LANE_GC, cum_t, jnp.where(grow < LANE_BETA, c[:, tm:], sigmoid))
    smallt_ref[0, :, rows] = small_t
    small = jnp.concatenate([small_t, jnp.zeros((LANES - GATE_ROWS, tm), F32)], axis=0).T
    small_ref[0, rows, :] = small

    pack_buf[sub] = _pack3(small * LOG2E, lane, one_lane=True)
    bias = _dot(pack_buf[sub], pmat_ref[...])
    pk = _dot(xn, w_ref[:, OFF_K:OFF_K + FOX_PAD])
    pv = _dot(xn, w_ref[:, OFF_V:OFF_V + FOX_WIDTH])
    pgg = _dot(xn, w_ref[:, OFF_GG:OFF_GG + GDN_WIDTH])
    for off, p, out_ref in ((OFF_Q, pq, qp_ref), (OFF_K, pk, kp_ref)):
        for h in range(FOX_HEADS):
            blk = p[:, h * LANES:(h + 1) * LANES]
            ms = jnp.sum(blk * blk, axis=-1, keepdims=True) * (1.0 / FOX_HEAD_DIM)
            nrm = blk * lax.rsqrt(ms + EPS)
            if off == OFF_Q:
                nrm = nrm * qg_ref[...]
            out_ref[0, rows, h * LANES:(h + 1) * LANES] = (
                nrm + bias[:, off + h * LANES:off + (h + 1) * LANES]).astype(BF16)

    v_t = pv.T.astype(BF16)
    for h in range(FOX_HEADS):
        vt_ref[0, sub, h * VT_ROWS:h * VT_ROWS + FOX_HEAD_DIM, :] = v_t[h * FOX_HEAD_DIM:(h + 1) * FOX_HEAD_DIM, :]
        vt_ref[0, sub, h * VT_ROWS + FOX_HEAD_DIM:(h + 1) * VT_ROWS, :] = jnp.ones((VT_ROWS - FOX_HEAD_DIM, tm), BF16)

    conv_buf[SUBLANES:SUBLANES + tm, :] = pg
    ext = conv_buf[...]
    y = cw_ref[CONV_WIDTH - 1:CONV_WIDTH, :] * pg
    for back in range(1, CONV_WIDTH):
        k = CONV_WIDTH - 1 - back
        y = y + cw_ref[k:k + 1, :] * pltpu.roll(ext, back, 0)[SUBLANES:SUBLANES + tm, :]
    conv_next[0:SUBLANES, :] = conv_buf[tm:tm + SUBLANES, :]
    y = _silu(y)
    for idx, out_ref in enumerate((gq_ref, gk_ref, gv_ref)):
        for h in range(GDN_HEADS):
            lo = idx * GDN_WIDTH + h * GDN_HEAD_DIM
            blk = y[:, lo:lo + GDN_HEAD_DIM]
            if idx < 2:
                blk = blk * lax.rsqrt(jnp.sum(blk * blk, axis=-1, keepdims=True) + EPS)
            if idx == 0:
                blk = blk * (GDN_HEAD_DIM ** -0.5)
            out_ref[0, rows, h * GDN_HEAD_DIM:(h + 1) * GDN_HEAD_DIM] = blk.astype(BF16)

    sgg_ref[0, rows, :] = _silu(pgg).astype(BF16)


def _proj_weights(w_in, fox_q_norm, fox_k_norm, fox_f_bias, gdn_a_log, gdn_dt_bias):
    d = w_in.shape[0]
    fw = FOX_WIDTH
    fq, fk, fv = w_in[:, 0:fw], w_in[:, fw:2 * fw], w_in[:, 2 * fw:3 * fw]
    o = 3 * fw
    ff = w_in[:, o:o + FOX_HEADS]
    o += FOX_HEADS
    gqkv = w_in[:, o:o + 3 * GDN_WIDTH]
    o += 3 * GDN_WIDTH
    ga = w_in[:, o:o + GDN_HEADS]
    gb = w_in[:, o + GDN_HEADS:o + 2 * GDN_HEADS]
    gg = w_in[:, o + 2 * GDN_HEADS:]

    def pad_heads(w):
        w = w.reshape(d, FOX_HEADS, FOX_HEAD_DIM)
        return jnp.pad(w, ((0, 0), (0, 0), (0, LANES - FOX_HEAD_DIM))).reshape(d, FOX_PAD)

    small = jnp.pad(jnp.concatenate([ff, ga, gb], axis=1), ((0, 0), (0, LANES - FOX_HEADS - 2 * GDN_HEADS)))
    w = jnp.concatenate([pad_heads(fq), pad_heads(fk), fv, gqkv, gg, small], axis=1).astype(BF16)

    def lane_row(vals, start):
        return jnp.zeros((1, LANES), F32).at[0, start:start + vals.shape[0]].set(vals.astype(F32))

    qg = lane_row(fox_q_norm * fox_k_norm * (FOX_HEAD_DIM ** -0.5 * LOG2E), 0)
    brow = lane_row(fox_f_bias, LANE_CUM) + lane_row(gdn_dt_bias, LANE_GC)
    alog = lane_row(gdn_a_log, LANE_GC)

    pm = jnp.zeros((LANES, 2, FOX_HEADS, LANES), F32)
    hh = jnp.arange(FOX_HEADS)
    for piece in range(3):
        src = piece * PIECE_LANES + LANE_CUM + hh
        pm = pm.at[src, 0, hh, BIAS_LANE + piece].set(1.0)
        pm = pm.at[src, 1, hh, BIAS_LANE + 3 + piece].set(-1.0)
        pm = pm.at[3 * PIECE_LANES, 0, hh, BIAS_LANE + 3 + piece].set(1.0)
        pm = pm.at[3 * PIECE_LANES, 1, hh, BIAS_LANE + piece].set(1.0)
    pmat = pm.reshape(LANES, 2 * FOX_PAD).astype(BF16)
    return w, qg, brow, alog, pmat


def _proj_call(x3d, w_in, fox_q_norm, fox_k_norm, fox_f_bias, gdn_conv, gdn_a_log, gdn_dt_bias):
    b, s, d = x3d.shape
    tm = min(PROJ_TILE_M, s)
    w, qg, brow, alog, pmat = _proj_weights(w_in, fox_q_norm, fox_k_norm, fox_f_bias, gdn_a_log, gdn_dt_bias)
    r = jnp.arange(tm)
    upper = r[:, None] <= r[None, :]
    same_chunk = (r[:, None] // GDN_CHUNK) == (r[None, :] // GDN_CHUNK)
    triu = jnp.concatenate([upper, upper & same_chunk], axis=1).astype(BF16)
    brow_t = jnp.broadcast_to(brow[0, :GATE_ROWS, None], (GATE_ROWS, tm))
    alog_t = jnp.broadcast_to(alog[0, :GATE_ROWS, None], (GATE_ROWS, tm))

    n_sub = PROJ_SUBTILES if s % (PROJ_SUBTILES * tm) == 0 else 1
    ts = n_sub * tm

    def tok(width, dtype):
        return (pl.BlockSpec((1, ts, width), lambda bi, si: (bi, si, 0)),
                jax.ShapeDtypeStruct((b, s, width), dtype))

    outs = [tok(FOX_PAD, BF16), tok(FOX_PAD, BF16),
            (pl.BlockSpec((1, n_sub, FOX_HEADS * VT_ROWS, tm), lambda bi, si: (bi, si, 0, 0)),
             jax.ShapeDtypeStruct((b, s // tm, FOX_HEADS * VT_ROWS, tm), BF16)),
            tok(GDN_WIDTH, BF16), tok(GDN_WIDTH, BF16), tok(GDN_WIDTH, BF16),
            tok(GDN_WIDTH, BF16), tok(LANES, F32),
            (pl.BlockSpec((1, GATE_ROWS, ts), lambda bi, si: (bi, 0, si)),
             jax.ShapeDtypeStruct((b, GATE_ROWS, s), F32))]
    consts = [w, gdn_conv.astype(F32), qg, brow_t, alog_t, triu, pmat]
    conv_scratch = pltpu.VMEM((tm + SUBLANES, 3 * GDN_WIDTH), F32)
    return pl.pallas_call(
        functools.partial(_proj_body, tm=tm, n_sub=n_sub),
        grid=(b, s // ts),
        in_specs=[pl.BlockSpec((1, ts, d), lambda bi, si: (bi, si, 0))] + [_const_spec(c.shape) for c in consts],
        out_specs=[o[0] for o in outs],
        out_shape=[o[1] for o in outs],
        scratch_shapes=[pltpu.VMEM((GATE_ROWS, LANES), F32), conv_scratch, conv_scratch,
                        pltpu.VMEM((n_sub, tm, LANES), BF16)],
        compiler_params=pltpu.CompilerParams(dimension_semantics=("arbitrary", "arbitrary"),
                                             vmem_limit_bytes=VMEM_LIMIT_BYTES),
        name="mixer_proj",
    )(x3d, *consts)


def _attn_body(q_ref, k_ref, vt_ref, o_ref, *, tile):
    i = pl.program_id(2)
    nh = ATTN_HEADS_PER_STEP
    dh = FOX_HEAD_DIM

    def logits(hh, j):
        off = pl.multiple_of(j * tile, tile)
        q = q_ref[0, :, hh * LANES:(hh + 1) * LANES]
        k = k_ref[0, pl.ds(off, tile), hh * LANES:(hh + 1) * LANES]
        return _dot_nt(k, q)

    def step(j, carry, diagonal):
        ahead = 2
        s_all = {hh: logits(hh, j) for hh in range(min(ahead, nh))}
        out = []
        for hh in range(nh):
            if hh + ahead < nh:
                s_all[hh + ahead] = logits(hh + ahead, j)
            m, acc = carry[hh]
            s = s_all.pop(hh)
            if diagonal:
                key = lax.broadcasted_iota(jnp.int32, s.shape, 0)
                qry = lax.broadcasted_iota(jnp.int32, s.shape, 1)
                s = jnp.where(qry >= key, s, -jnp.inf)
            m_new = jnp.maximum(m, jnp.max(s, axis=0, keepdims=True))
            p = jnp.exp2(s - m_new).astype(BF16)
            vt = vt_ref[0, j, hh * VT_ROWS:(hh + 1) * VT_ROWS, :]
            acc = jnp.exp2(m - m_new) * acc + _dot(vt, p)
            out.append((m_new, acc))
        return tuple(out)

    init = tuple((jnp.full((1, tile), -jnp.inf, F32), jnp.zeros((VT_ROWS, tile), F32)) for _ in range(nh))
    carry = lax.fori_loop(0, i, lambda j, c: step(j, c, False), init)
    carry = step(i, carry, True)
    out_t = jnp.concatenate([acc[:dh] / acc[dh:dh + 1] for _, acc in carry], axis=0)
    o_ref[0] = out_t.T.astype(BF16)


def _attn_call(qp, kp, vt):
    b, s, _ = qp.shape
    tile = min(ATTN_TILE, s)
    nh = ATTN_HEADS_PER_STEP
    return pl.pallas_call(
        functools.partial(_attn_body, tile=tile),
        grid=(b, FOX_HEADS // nh, s // tile),
        in_specs=[pl.BlockSpec((1, tile, nh * LANES), lambda bi, hi, qi: (bi, qi, hi)),
                  pl.BlockSpec((1, s, nh * LANES), lambda bi, hi, qi: (bi, 0, hi)),
                  pl.BlockSpec((1, s // tile, nh * VT_ROWS, tile), lambda bi, hi, qi: (bi, 0, hi, 0))],
        out_specs=pl.BlockSpec((1, tile, nh * FOX_HEAD_DIM), lambda bi, hi, qi: (bi, qi, hi)),
        out_shape=jax.ShapeDtypeStruct((b, s, FOX_WIDTH), BF16),
        compiler_params=pltpu.CompilerParams(dimension_semantics=("arbitrary", "arbitrary", "arbitrary"),
                                             vmem_limit_bytes=VMEM_LIMIT_BYTES),
        name="fox_attention",
    )(qp, kp, vt)


def _gdn_body(q_ref, k_ref, v_ref, sgg_ref, small_ref, smallt_ref, gain_ref, y_ref, state_ref, *, n_chunks):
    @pl.when(pl.program_id(1) == 0)
    def _():
        state_ref[...] = jnp.zeros_like(state_ref)

    n = GDN_CHUNK
    row = lax.broadcasted_iota(jnp.int32, (n, n), 0)
    col = lax.broadcasted_iota(jnp.int32, (n, n), 1)
    incl = row >= col
    strict = row > col
    eye = (row == col).astype(F32)

    pairs = [(c, h) for c in range(n_chunks) for h in range(GDN_HEADS)]
    pre = []
    for c, h in pairs:
        rows = slice(c * n, (c + 1) * n)
        hs = slice(h * GDN_HEAD_DIM, (h + 1) * GDN_HEAD_DIM)
        sm = small_ref[0, rows, :]
        smt = smallt_ref[0, :, rows]
        gcc = jnp.sum(jnp.where(col == LANE_GC + h, sm, 0.0), axis=1, keepdims=True)
        beta = jnp.sum(jnp.where(col == LANE_BETA + h, sm, 0.0), axis=1, keepdims=True)
        gcr = jnp.sum(jnp.where(row[:GATE_ROWS] == LANE_GC + h, smt, 0.0), axis=0, keepdims=True)
        g_last = jnp.sum(jnp.where(col[0:1, :] == n - 1, gcr, 0.0), axis=1, keepdims=True)
        k_b = k_ref[0, rows, hs]
        k = k_b.astype(F32)
        kb = k * beta
        pre.append(dict(rows=rows, hs=hs, gcc=gcc, g_last=g_last, k_b=k_b, k=k, kb=kb,
                        q_b=q_ref[0, rows, hs], vb=v_ref[0, rows, hs].astype(F32) * beta,
                        decay=jnp.exp(jnp.where(incl, gcc - gcr, -jnp.inf)), e_g=jnp.exp(gcc)))

    x_pow = [jnp.where(strict, -(_dot_nt(p["kb"].astype(BF16), p["k_b"]) * p["decay"]), 0.0) for p in pre]
    t = [eye + x for x in x_pow]
    x_pow = [_dot(x.astype(BF16), x.astype(BF16)) for x in x_pow]
    for step in range(1, NEUMANN_STEPS + 1):
        if step < NEUMANN_STEPS:
            x_b = [x.astype(BF16) for x in x_pow]
            both = [_dot(jnp.concatenate([xb, ti.astype(BF16)], axis=0), xb) for xb, ti in zip(x_b, t)]
            x_pow = [r[:n] for r in both]
            t = [ti + r[n:] for ti, r in zip(t, both)]
        else:
            t = [ti + _dot(ti.astype(BF16), x.astype(BF16)) for x, ti in zip(x_pow, t)]

    uw = [_dot(ti.astype(BF16), jnp.concatenate([p["vb"], p["kb"] * p["e_g"]], axis=1).astype(BF16))
          for ti, p in zip(t, pre)]
    attn = [jnp.where(incl, _dot_nt(p["q_b"], p["k_b"]) * p["decay"], 0.0).astype(BF16) for p in pre]

    states = [state_ref[h] for h in range(GDN_HEADS)]
    for c in range(n_chunks):
        idx = [c * GDN_HEADS + h for h in range(GDN_HEADS)]
        st_b = [st.astype(BF16) for st in states]
        v_new = [(uw[i][:, :GDN_HEAD_DIM] - _dot(uw[i][:, GDN_HEAD_DIM:].astype(BF16), st_b[h])).astype(BF16)
                 for h, i in enumerate(idx)]
        new_states = []
        for h, i in enumerate(idx):
            p = pre[i]
            kd_t = (p["k"] * jnp.exp(p["g_last"] - p["gcc"])).T.astype(BF16)
            new_states.append(states[h] * jnp.exp(p["g_last"]) + _dot(kd_t, v_new[h]))
        for h, i in enumerate(idx):
            p = pre[i]
            qd = (p["q_b"].astype(F32) * p["e_g"]).astype(BF16)
            o = _dot(jnp.concatenate([qd, attn[i]], axis=1), jnp.concatenate([st_b[h], v_new[h]], axis=0))
            y = _rms_norm(o, gain_ref[...]) * sgg_ref[0, p["rows"], p["hs"]].astype(F32)
            y_ref[0, p["rows"], p["hs"]] = y.astype(BF16)
        states = new_states

    for h in range(GDN_HEADS):
        state_ref[h] = states[h]


def _gdn_call(gq, gk, gv, sgg, small, small_t, out_gain):
    b, s, _ = gq.shape
    n_chunks = min(GDN_CHUNKS_PER_STEP, s // GDN_CHUNK)
    ts = n_chunks * GDN_CHUNK
    tok = pl.BlockSpec((1, ts, GDN_WIDTH), lambda bi, ci: (bi, ci, 0))
    return pl.pallas_call(
        functools.partial(_gdn_body, n_chunks=n_chunks),
        grid=(b, s // ts),
        in_specs=[tok, tok, tok, tok,
                  pl.BlockSpec((1, ts, LANES), lambda bi, ci: (bi, ci, 0)),
                  pl.BlockSpec((1, GATE_ROWS, ts), lambda bi, ci: (bi, 0, ci)),
                  _const_spec((1, GDN_HEAD_DIM))],
        out_specs=tok,
        out_shape=jax.ShapeDtypeStruct((b, s, GDN_WIDTH), BF16),
        scratch_shapes=[pltpu.VMEM((GDN_HEADS, GDN_HEAD_DIM, GDN_HEAD_DIM), F32)],
        compiler_params=pltpu.CompilerParams(dimension_semantics=("arbitrary", "arbitrary"),
                                             vmem_limit_bytes=VMEM_LIMIT_BYTES),
        name="gated_delta_rule",
    )(gq, gk, gv, sgg, small, small_t, out_gain.reshape(1, GDN_HEAD_DIM).astype(F32))


def kernel(x, ffn1_norm, ffn1_w_in, ffn1_w_out, mix_norm, w_in, fox_q_norm, fox_k_norm, fox_f_bias,
           gdn_conv, gdn_a_log, gdn_dt_bias, gdn_out_norm, w_out, ffn2_norm, ffn2_w_in, ffn2_w_out):
    b, s, d = x.shape
    depth = ffn1_norm.shape[0]
    h = x.reshape(b * s, d)
    for l in range(depth):
        h, hn = _ffn_call(h, ffn1_norm[l], ffn1_w_in[l], ffn1_w_out[l], next_gain=mix_norm[l])
        qp, kp, vt, gq, gk, gv, sgg, small, small_t = _proj_call(
            hn.reshape(b, s, d), w_in[l], fox_q_norm[l], fox_k_norm[l], fox_f_bias[l],
            gdn_conv[l], gdn_a_log[l], gdn_dt_bias[l])
        y_fox = _attn_call(qp, kp, vt)
        y_gdn = _gdn_call(gq, gk, gv, sgg, small, small_t, gdn_out_norm[l])
        wo = w_out[l].astype(BF16)
        h = _ffn_call(h, ffn2_norm[l], ffn2_w_in[l], ffn2_w_out[l],
                      mix=(y_fox.reshape(b * s, FOX_WIDTH), y_gdn.reshape(b * s, GDN_WIDTH),
                           wo[:FOX_WIDTH], wo[FOX_WIDTH:]))
    return h.reshape(b, s, d)
```

```python
import functools

import jax
import jax.numpy as jnp
from jax import lax
from jax.experimental import pallas as pl
from jax.experimental.pallas import tpu as pltpu

F32 = jnp.float32
BF16 = jnp.bfloat16

D_MODEL = 1024
FOX_HEADS = 8
FOX_HEAD_DIM = 64
FOX_WIDTH = FOX_HEADS * FOX_HEAD_DIM
GDN_HEADS = 4
GDN_HEAD_DIM = 128
GDN_WIDTH = GDN_HEADS * GDN_HEAD_DIM
CONV_WIDTH = 4
D_FF = 2816
EPS = 1e-6

LANES = 128
SUBLANES = 8
VMEM_LIMIT_BYTES = 56 * 1024 * 1024

FOX_PAD = FOX_HEADS * LANES
OFF_Q = 0
OFF_K = OFF_Q + FOX_PAD
OFF_V = OFF_K + FOX_PAD
OFF_G = OFF_V + FOX_WIDTH
OFF_GG = OFF_G + 3 * GDN_WIDTH
OFF_SMALL = OFF_GG + GDN_WIDTH
N_PROJ = OFF_SMALL + LANES

LANE_CUM = 0
LANE_GC = FOX_HEADS
LANE_BETA = LANE_GC + GDN_HEADS
GATE_ROWS = 16
PIECE_LANES = GATE_ROWS

BIAS_LANE = FOX_HEAD_DIM
LOG2E = 1.4426950408889634
BF16_SUBLANES = 16
VT_ROWS = FOX_HEAD_DIM + BF16_SUBLANES

GDN_CHUNK = LANES
NEUMANN_STEPS = 6
GDN_CHUNKS_PER_STEP = 8

FFN_SUBTILE_M = 512
FFN_TILE_M = 2 * FFN_SUBTILE_M
FFN_TILE_F = 256
ATTN_TILE = 512
PROJ_TILE_M = ATTN_TILE
PROJ_SUBTILES = 2
ATTN_HEADS_PER_STEP = 8


def _dot(a, b):
    return jnp.dot(a, b, preferred_element_type=F32)


def _dot_nt(a, b):
    return lax.dot_general(a, b, (((1,), (1,)), ((), ())), preferred_element_type=F32)


def _keep_bf16_bits(v):
    bits = lax.bitcast_convert_type(v, jnp.uint32) & jnp.uint32(0xFFFF0000)
    return lax.bitcast_convert_type(bits, F32)


def _split3(v):
    hi = _keep_bf16_bits(v)
    r1 = v - hi
    mid = _keep_bf16_bits(r1)
    lo = _keep_bf16_bits(r1 - mid)
    return hi, mid, lo


def _pack3(v, lane, one_lane=False):
    hi, mid, lo = _split3(v)
    tail = jnp.where(lane == 3 * PIECE_LANES, 1.0, 0.0) if one_lane else 0.0
    packed = jnp.where(lane < PIECE_LANES, hi,
                       jnp.where(lane < 2 * PIECE_LANES, pltpu.roll(mid, PIECE_LANES, 1),
                                 jnp.where(lane < 3 * PIECE_LANES, pltpu.roll(lo, 2 * PIECE_LANES, 1), tail)))
    return packed.astype(BF16)


def _rms_norm(x, gain_row):
    ms = jnp.mean(x * x, axis=-1, keepdims=True)
    return x * lax.rsqrt(ms + EPS) * gain_row


def _silu(x):
    return x / (1.0 + jnp.exp(-x))


def _const_spec(shape):
    nd = len(shape)
    return pl.BlockSpec(shape, lambda *_: (0,) * nd, pipeline_mode=pl.Buffered(1))


def _ffn_body(*refs, n_chunks, tile_f, has_mix):
    if has_mix:
        x_ref, yf_ref, yg_ref, wof_ref, wog_ref, g_ref, win_ref, wout_ref, o_ref = refs
    else:
        x_ref, g_ref, win_ref, wout_ref, gnext_ref, o_ref, xn_next_ref = refs
    for r in range(x_ref.shape[0] // FFN_SUBTILE_M):
        rows = slice(r * FFN_SUBTILE_M, (r + 1) * FFN_SUBTILE_M)
        x = x_ref[rows, :]
        if has_mix:
            x = x + _dot(yf_ref[rows, :], wof_ref[...]) + _dot(yg_ref[rows, :], wog_ref[...])
        xn = _rms_norm(x, g_ref[...]).astype(BF16)
        acc = None
        for c in range(n_chunks):
            cols = slice(c * tile_f, (c + 1) * tile_f)
            gate = _dot(xn, win_ref[:, cols])
            up = _dot(xn, win_ref[:, D_FF + c * tile_f:D_FF + (c + 1) * tile_f])
            act = (_silu(gate) * up).astype(BF16)
            part = _dot(act, wout_ref[cols, :])
            acc = part if acc is None else acc + part
        out = x + 0.5 * acc
        o_ref[rows, :] = out
        if not has_mix:
            xn_next_ref[rows, :] = _rms_norm(out, gnext_ref[...]).astype(BF16)


def _ffn_call(x2d, gain, w_in, w_out, mix=None, next_gain=None):
    m, d = x2d.shape
    tm = FFN_TILE_M if m % FFN_TILE_M == 0 else FFN_SUBTILE_M
    assert m % tm == 0, (m, tm)
    tf = FFN_TILE_F
    nc = D_FF // tf
    w_in_r = w_in.astype(BF16)
    w_out_r = w_out.astype(BF16)
    row = pl.BlockSpec((tm, d), lambda i: (i, 0))
    in_specs = [row]
    args = [x2d]
    if mix is not None:
        y_fox, y_gdn, wo_f, wo_g = mix
        in_specs += [pl.BlockSpec((tm, FOX_WIDTH), lambda i: (i, 0)),
                     pl.BlockSpec((tm, GDN_WIDTH), lambda i: (i, 0)),
                     _const_spec(wo_f.shape), _const_spec(wo_g.shape)]
        args += [y_fox, y_gdn, wo_f, wo_g]
    in_specs += [_const_spec((1, d)), _const_spec(w_in_r.shape), _const_spec(w_out_r.shape)]
    args += [gain.reshape(1, d), w_in_r, w_out_r]
    out_specs = row
    out_shape = jax.ShapeDtypeStruct((m, d), F32)
    if mix is None:
        in_specs += [_const_spec((1, d))]
        args += [next_gain.reshape(1, d)]
        out_specs = [row, row]
        out_shape = [out_shape, jax.ShapeDtypeStruct((m, d), BF16)]
    return pl.pallas_call(
        functools.partial(_ffn_body, n_chunks=nc, tile_f=tf, has_mix=mix is not None),
        grid=(m // tm,),
        in_specs=in_specs,
        out_specs=out_specs,
        out_shape=out_shape,
        compiler_params=pltpu.CompilerParams(dimension_semantics=("arbitrary",),
                                             vmem_limit_bytes=VMEM_LIMIT_BYTES),
        name="ffn_mix" if mix is not None else "ffn",
    )(*args)


def _proj_body(xn_ref, w_ref, cw_ref, qg_ref, browt_ref, alogt_ref, triu_ref, pmat_ref,
               qp_ref, kp_ref, vt_ref, gq_ref, gk_ref, gv_ref, sgg_ref, small_ref, smallt_ref,
               cum_carry, conv_a, conv_b, pack_buf, *, tm, n_sub):
    @pl.when(pl.program_id(1) == 0)
    def _():
        cum_carry[...] = jnp.zeros_like(cum_carry)
        conv_a[0:SUBLANES, :] = jnp.zeros((SUBLANES, 3 * GDN_WIDTH), F32)

    conv_bufs = (conv_a, conv_b)
    for sub in range(n_sub):
        _proj_tile(sub, slice(sub * tm, (sub + 1) * tm), conv_bufs[sub], conv_bufs[(sub + 1) % n_sub],
                   xn_ref, w_ref, cw_ref, qg_ref, browt_ref, alogt_ref, triu_ref, pmat_ref,
                   qp_ref, kp_ref, vt_ref, gq_ref, gk_ref, gv_ref, sgg_ref, small_ref, smallt_ref,
                   cum_carry, pack_buf, tm)


def _proj_tile(sub, rows, conv_buf, conv_next, xn_ref, w_ref, cw_ref, qg_ref, browt_ref, alogt_ref, triu_ref,
               pmat_ref, qp_ref, kp_ref, vt_ref, gq_ref, gk_ref, gv_ref, sgg_ref, small_ref, smallt_ref,
               cum_carry, pack_buf, tm):
    xn = xn_ref[0, rows, :]
    lane = lax.broadcasted_iota(jnp.int32, (tm, LANES), 1)

    z = _dot(xn, w_ref[:, OFF_SMALL:OFF_SMALL + LANES])
    pg = _dot(xn, w_ref[:, OFF_G:OFF_G + 3 * GDN_WIDTH])

    zt = z.T[0:GATE_ROWS] + browt_ref[...]
    grow = lax.broadcasted_iota(jnp.int32, (GATE_ROWS, tm), 0)
    e = jnp.log1p(jnp.exp(-jnp.abs(zt)))
    log_sig = jnp.minimum(zt, 0.0) - e
    softplus = jnp.maximum(zt, 0.0) + e
    sigmoid = 1.0 / (1.0 + jnp.exp(-zt))
    log_decay = -jnp.exp(alogt_ref[...]) * softplus
    pieces = jnp.concatenate(_split3(jnp.where(grow < LANE_GC, log_sig, log_decay)), axis=0).astype(BF16)
    c3 = _dot(pieces, triu_ref[...])
    c = c3[0:GATE_ROWS] + c3[GATE_ROWS:2 * GATE_ROWS] + c3[2 * GATE_ROWS:]
    cum_t = c[:, :tm] + jnp.concatenate([cum_carry[...]] * (tm // LANES), axis=1)
    pq = _dot(xn, w_ref[:, OFF_Q:OFF_Q + FOX_PAD])
    cum_carry[...] = jnp.broadcast_to(cum_t[:, tm - 1:tm], (GATE_ROWS, LANES))
    small_t = jnp.where(grow < LANE_GC, cum_t, jnp.where(grow < LANE_BETA, c[:, tm:], sigmoid))
    smallt_ref[0, :, rows] = small_t
    small = jnp.concatenate([small_t, jnp.zeros((LANES - GATE_ROWS, tm), F32)], axis=0).T
    small_ref[0, rows, :] = small

    pack_buf[sub] = _pack3(small * LOG2E, lane, one_lane=True)
    bias = _dot(pack_buf[sub], pmat_ref[...])
    pk = _dot(xn, w_ref[:, OFF_K:OFF_K + FOX_PAD])
    pv = _dot(xn, w_ref[:, OFF_V:OFF_V + FOX_WIDTH])
    pgg = _dot(xn, w_ref[:, OFF_GG:OFF_GG + GDN_WIDTH])
    for off, p, out_ref in ((OFF_Q, pq, qp_ref), (OFF_K, pk, kp_ref)):
        for h in range(FOX_HEADS):
            blk = p[:, h * LANES:(h + 1) * LANES]
            ms = jnp.sum(blk * blk, axis=-1, keepdims=True) * (1.0 / FOX_HEAD_DIM)
            nrm = blk * lax.rsqrt(ms + EPS)
            if off == OFF_Q:
                full = nrm * qg_ref[...] + bias[:, off + h * LANES:off + (h + 1) * LANES]
                out_ref[0, sub, h * LANES:(h + 1) * LANES, :] = full.T.astype(BF16)
            else:
                out_ref[0, rows, h * LANES:(h + 1) * LANES] = (
                    nrm + bias[:, off + h * LANES:off + (h + 1) * LANES]).astype(BF16)

    v_t = pv.T.astype(BF16)
    for h in range(FOX_HEADS):
        vt_ref[0, sub, h * VT_ROWS:h * VT_ROWS + FOX_HEAD_DIM, :] = v_t[h * FOX_HEAD_DIM:(h + 1) * FOX_HEAD_DIM, :]
        vt_ref[0, sub, h * VT_ROWS + FOX_HEAD_DIM:(h + 1) * VT_ROWS, :] = jnp.ones((VT_ROWS - FOX_HEAD_DIM, tm), BF16)

    conv_buf[SUBLANES:SUBLANES + tm, :] = pg
    ext = conv_buf[...]
    y = cw_ref[CONV_WIDTH - 1:CONV_WIDTH, :] * pg
    for back in range(1, CONV_WIDTH):
        k = CONV_WIDTH - 1 - back
        y = y + cw_ref[k:k + 1, :] * pltpu.roll(ext, back, 0)[SUBLANES:SUBLANES + tm, :]
    conv_next[0:SUBLANES, :] = conv_buf[tm:tm + SUBLANES, :]
    y = _silu(y)
    for idx, out_ref in enumerate((gq_ref, gk_ref, gv_ref)):
        for h in range(GDN_HEADS):
            lo = idx * GDN_WIDTH + h * GDN_HEAD_DIM
            blk = y[:, lo:lo + GDN_HEAD_DIM]
            if idx < 2:
                blk = blk * lax.rsqrt(jnp.sum(blk * blk, axis=-1, keepdims=True) + EPS)
            if idx == 0:
                blk = blk * (GDN_HEAD_DIM ** -0.5)
            out_ref[0, rows, h * GDN_HEAD_DIM:(h + 1) * GDN_HEAD_DIM] = blk.astype(BF16)

    sgg_ref[0, rows, :] = _silu(pgg).astype(BF16)


def _proj_weights(w_in, fox_q_norm, fox_k_norm, fox_f_bias, gdn_a_log, gdn_dt_bias):
    d = w_in.shape[0]
    fw = FOX_WIDTH
    fq, fk, fv = w_in[:, 0:fw], w_in[:, fw:2 * fw], w_in[:, 2 * fw:3 * fw]
    o = 3 * fw
    ff = w_in[:, o:o + FOX_HEADS]
    o += FOX_HEADS
    gqkv = w_in[:, o:o + 3 * GDN_WIDTH]
    o += 3 * GDN_WIDTH
    ga = w_in[:, o:o + GDN_HEADS]
    gb = w_in[:, o + GDN_HEADS:o + 2 * GDN_HEADS]
    gg = w_in[:, o + 2 * GDN_HEADS:]

    def pad_heads(w):
        w = w.reshape(d, FOX_HEADS, FOX_HEAD_DIM)
        return jnp.pad(w, ((0, 0), (0, 0), (0, LANES - FOX_HEAD_DIM))).reshape(d, FOX_PAD)

    small = jnp.pad(jnp.concatenate([ff, ga, gb], axis=1), ((0, 0), (0, LANES - FOX_HEADS - 2 * GDN_HEADS)))
    w = jnp.concatenate([pad_heads(fq), pad_heads(fk), fv, gqkv, gg, small], axis=1).astype(BF16)

    def lane_row(vals, start):
        return jnp.zeros((1, LANES), F32).at[0, start:start + vals.shape[0]].set(vals.astype(F32))

    qg = lane_row(fox_q_norm * fox_k_norm * (FOX_HEAD_DIM ** -0.5 * LOG2E), 0)
    brow = lane_row(fox_f_bias, LANE_CUM) + lane_row(gdn_dt_bias, LANE_GC)
    alog = lane_row(gdn_a_log, LANE_GC)

    pm = jnp.zeros((LANES, 2, FOX_HEADS, LANES), F32)
    hh = jnp.arange(FOX_HEADS)
    for piece in range(3):
        src = piece * PIECE_LANES + LANE_CUM + hh
        pm = pm.at[src, 0, hh, BIAS_LANE + piece].set(1.0)
        pm = pm.at[src, 1, hh, BIAS_LANE + 3 + piece].set(-1.0)
        pm = pm.at[3 * PIECE_LANES, 0, hh, BIAS_LANE + 3 + piece].set(1.0)
        pm = pm.at[3 * PIECE_LANES, 1, hh, BIAS_LANE + piece].set(1.0)
    pmat = pm.reshape(LANES, 2 * FOX_PAD).astype(BF16)
    return w, qg, brow, alog, pmat


def _proj_call(x3d, w_in, fox_q_norm, fox_k_norm, fox_f_bias, gdn_conv, gdn_a_log, gdn_dt_bias):
    b, s, d = x3d.shape
    tm = min(PROJ_TILE_M, s)
    w, qg, brow, alog, pmat = _proj_weights(w_in, fox_q_norm, fox_k_norm, fox_f_bias, gdn_a_log, gdn_dt_bias)
    r = jnp.arange(tm)
    upper = r[:, None] <= r[None, :]
    same_chunk = (r[:, None] // GDN_CHUNK) == (r[None, :] // GDN_CHUNK)
    triu = jnp.concatenate([upper, upper & same_chunk], axis=1).astype(BF16)
    brow_t = jnp.broadcast_to(brow[0, :GATE_ROWS, None], (GATE_ROWS, tm))
    alog_t = jnp.broadcast_to(alog[0, :GATE_ROWS, None], (GATE_ROWS, tm))

    n_sub = PROJ_SUBTILES if s % (PROJ_SUBTILES * tm) == 0 else 1
    ts = n_sub * tm

    def tok(width, dtype):
        return (pl.BlockSpec((1, ts, width), lambda bi, si: (bi, si, 0)),
                jax.ShapeDtypeStruct((b, s, width), dtype))

    outs = [(pl.BlockSpec((1, n_sub, FOX_PAD, tm), lambda bi, si: (bi, si, 0, 0)),
             jax.ShapeDtypeStruct((b, s // tm, FOX_PAD, tm), BF16)),
            tok(FOX_PAD, BF16),
            (pl.BlockSpec((1, n_sub, FOX_HEADS * VT_ROWS, tm), lambda bi, si: (bi, si, 0, 0)),
             jax.ShapeDtypeStruct((b, s // tm, FOX_HEADS * VT_ROWS, tm), BF16)),
            tok(GDN_WIDTH, BF16), tok(GDN_WIDTH, BF16), tok(GDN_WIDTH, BF16),
            tok(GDN_WIDTH, BF16), tok(LANES, F32),
            (pl.BlockSpec((1, GATE_ROWS, ts), lambda bi, si: (bi, 0, si)),
             jax.ShapeDtypeStruct((b, GATE_ROWS, s), F32))]
    consts = [w, gdn_conv.astype(F32), qg, brow_t, alog_t, triu, pmat]
    conv_scratch = pltpu.VMEM((tm + SUBLANES, 3 * GDN_WIDTH), F32)
    return pl.pallas_call(
        functools.partial(_proj_body, tm=tm, n_sub=n_sub),
        grid=(b, s // ts),
        in_specs=[pl.BlockSpec((1, ts, d), lambda bi, si: (bi, si, 0))] + [_const_spec(c.shape) for c in consts],
        out_specs=[o[0] for o in outs],
        out_shape=[o[1] for o in outs],
        scratch_shapes=[pltpu.VMEM((GATE_ROWS, LANES), F32), conv_scratch, conv_scratch,
                        pltpu.VMEM((n_sub, tm, LANES), BF16)],
        compiler_params=pltpu.CompilerParams(dimension_semantics=("arbitrary", "arbitrary"),
                                             vmem_limit_bytes=VMEM_LIMIT_BYTES),
        name="mixer_proj",
    )(x3d, *consts)


def _attn_body(q_ref, k_ref, vt_ref, o_ref, *, tile):
    i = pl.program_id(2)
    nh = ATTN_HEADS_PER_STEP
    dh = FOX_HEAD_DIM

    def logits(hh, j):
        off = pl.multiple_of(j * tile, tile)
        q_t = q_ref[0, 0, hh * LANES:(hh + 1) * LANES, :]
        k = k_ref[0, pl.ds(off, tile), hh * LANES:(hh + 1) * LANES]
        return _dot(k, q_t)

    def step(j, carry, diagonal):
        ahead = 2
        s_all = {hh: logits(hh, j) for hh in range(min(ahead, nh))}
        out = []
        for hh in range(nh):
            if hh + ahead < nh:
                s_all[hh + ahead] = logits(hh + ahead, j)
            m, acc = carry[hh]
            s = s_all.pop(hh)
            if diagonal:
                key = lax.broadcasted_iota(jnp.int32, s.shape, 0)
                qry = lax.broadcasted_iota(jnp.int32, s.shape, 1)
                s = jnp.where(qry >= key, s, -jnp.inf)
            m_new = jnp.maximum(m, jnp.max(s, axis=0, keepdims=True))
            p = jnp.exp2(s - m_new).astype(BF16)
            vt = vt_ref[0, j, hh * VT_ROWS:(hh + 1) * VT_ROWS, :]
            acc = jnp.exp2(m - m_new) * acc + _dot(vt, p)
            out.append((m_new, acc))
        return tuple(out)

    init = tuple((jnp.full((1, tile), -jnp.inf, F32), jnp.zeros((VT_ROWS, tile), F32)) for _ in range(nh))
    carry = lax.fori_loop(0, i, lambda j, c: step(j, c, False), init)
    carry = step(i, carry, True)
    out_t = jnp.concatenate([acc[:dh] / acc[dh:dh + 1] for _, acc in carry], axis=0)
    o_ref[0] = out_t.T.astype(BF16)


def _attn_call(qp, kp, vt):
    b, s, _ = kp.shape
    tile = min(ATTN_TILE, s)
    nh = ATTN_HEADS_PER_STEP
    return pl.pallas_call(
        functools.partial(_attn_body, tile=tile),
        grid=(b, FOX_HEADS // nh, s // tile),
        in_specs=[pl.BlockSpec((1, 1, nh * LANES, tile), lambda bi, hi, qi: (bi, qi, hi, 0)),
                  pl.BlockSpec((1, s, nh * LANES), lambda bi, hi, qi: (bi, 0, hi)),
                  pl.BlockSpec((1, s // tile, nh * VT_ROWS, tile), lambda bi, hi, qi: (bi, 0, hi, 0))],
        out_specs=pl.BlockSpec((1, tile, nh * FOX_HEAD_DIM), lambda bi, hi, qi: (bi, qi, hi)),
        out_shape=jax.ShapeDtypeStruct((b, s, FOX_WIDTH), BF16),
        compiler_params=pltpu.CompilerParams(dimension_semantics=("arbitrary", "arbitrary", "arbitrary"),
                                             vmem_limit_bytes=VMEM_LIMIT_BYTES),
        name="fox_attention",
    )(qp, kp, vt)


def _gdn_body(q_ref, k_ref, v_ref, sgg_ref, small_ref, smallt_ref, gain_ref, y_ref, state_ref, *, n_chunks):
    @pl.when(pl.program_id(1) == 0)
    def _():
        state_ref[...] = jnp.zeros_like(state_ref)

    n = GDN_CHUNK
    row = lax.broadcasted_iota(jnp.int32, (n, n), 0)
    col = lax.broadcasted_iota(jnp.int32, (n, n), 1)
    incl = row >= col
    strict = row > col
    eye = (row == col).astype(F32)

    pairs = [(c, h) for c in range(n_chunks) for h in range(GDN_HEADS)]
    pre = []
    for c, h in pairs:
        rows = slice(c * n, (c + 1) * n)
        hs = slice(h * GDN_HEAD_DIM, (h + 1) * GDN_HEAD_DIM)
        sm = small_ref[0, rows, :]
        smt = smallt_ref[0, :, rows]
        gcc = jnp.sum(jnp.where(col == LANE_GC + h, sm, 0.0), axis=1, keepdims=True)
        beta = jnp.sum(jnp.where(col == LANE_BETA + h, sm, 0.0), axis=1, keepdims=True)
        gcr = jnp.sum(jnp.where(row[:GATE_ROWS] == LANE_GC + h, smt, 0.0), axis=0, keepdims=True)
        g_last = jnp.sum(jnp.where(col[0:1, :] == n - 1, gcr, 0.0), axis=1, keepdims=True)
        k_b = k_ref[0, rows, hs]
        k = k_b.astype(F32)
        kb = k * beta
        pre.append(dict(rows=rows, hs=hs, gcc=gcc, g_last=g_last, k_b=k_b, k=k, kb=kb,
                        q_b=q_ref[0, rows, hs], vb=v_ref[0, rows, hs].astype(F32) * beta,
                        decay=jnp.exp(jnp.where(incl, gcc - gcr, -jnp.inf)), e_g=jnp.exp(gcc)))

    x_pow = [jnp.where(strict, -(_dot_nt(p["kb"].astype(BF16), p["k_b"]) * p["decay"]), 0.0) for p in pre]
    t = [eye + x for x in x_pow]
    x_pow = [_dot(x.astype(BF16), x.astype(BF16)) for x in x_pow]
    for step in range(1, NEUMANN_STEPS + 1):
        if step < NEUMANN_STEPS:
            x_b = [x.astype(BF16) for x in x_pow]
            both = [_dot(jnp.concatenate([xb, ti.astype(BF16)], axis=0), xb) for xb, ti in zip(x_b, t)]
            x_pow = [r[:n] for r in both]
            t = [ti + r[n:] for ti, r in zip(t, both)]
        else:
            t = [ti + _dot(ti.astype(BF16), x.astype(BF16)) for x, ti in zip(x_pow, t)]

    uw = [_dot(ti.astype(BF16), jnp.concatenate([p["vb"], p["kb"] * p["e_g"]], axis=1).astype(BF16))
          for ti, p in zip(t, pre)]
    attn = [jnp.where(incl, _dot_nt(p["q_b"], p["k_b"]) * p["decay"], 0.0).astype(BF16) for p in pre]

    states = [state_ref[h] for h in range(GDN_HEADS)]
    for c in range(n_chunks):
        idx = [c * GDN_HEADS + h for h in range(GDN_HEADS)]
        st_b = [st.astype(BF16) for st in states]
        v_new = [(uw[i][:, :GDN_HEAD_DIM] - _dot(uw[i][:, GDN_HEAD_DIM:].astype(BF16), st_b[h])).astype(BF16)
                 for h, i in enumerate(idx)]
        new_states = []
        for h, i in enumerate(idx):
            p = pre[i]
            kd_t = (p["k"] * jnp.exp(p["g_last"] - p["gcc"])).T.astype(BF16)
            new_states.append(states[h] * jnp.exp(p["g_last"]) + _dot(kd_t, v_new[h]))
        for h, i in enumerate(idx):
            p = pre[i]
            qd = (p["q_b"].astype(F32) * p["e_g"]).astype(BF16)
            o = _dot(jnp.concatenate([qd, attn[i]], axis=1), jnp.concatenate([st_b[h], v_new[h]], axis=0))
            y = _rms_norm(o, gain_ref[...]) * sgg_ref[0, p["rows"], p["hs"]].astype(F32)
            y_ref[0, p["rows"], p["hs"]] = y.astype(BF16)
        states = new_states

    for h in range(GDN_HEADS):
        state_ref[h] = states[h]


def _gdn_call(gq, gk, gv, sgg, small, small_t, out_gain):
    b, s, _ = gq.shape
    n_chunks = min(GDN_CHUNKS_PER_STEP, s // GDN_CHUNK)
    ts = n_chunks * GDN_CHUNK
    tok = pl.BlockSpec((1, ts, GDN_WIDTH), lambda bi, ci: (bi, ci, 0))
    return pl.pallas_call(
        functools.partial(_gdn_body, n_chunks=n_chunks),
        grid=(b, s // ts),
        in_specs=[tok, tok, tok, tok,
                  pl.BlockSpec((1, ts, LANES), lambda bi, ci: (bi, ci, 0)),
                  pl.BlockSpec((1, GATE_ROWS, ts), lambda bi, ci: (bi, 0, ci)),
                  _const_spec((1, GDN_HEAD_DIM))],
        out_specs=tok,
        out_shape=jax.ShapeDtypeStruct((b, s, GDN_WIDTH), BF16),
        scratch_shapes=[pltpu.VMEM((GDN_HEADS, GDN_HEAD_DIM, GDN_HEAD_DIM), F32)],
        compiler_params=pltpu.CompilerParams(dimension_semantics=("arbitrary", "arbitrary"),
                                             vmem_limit_bytes=VMEM_LIMIT_BYTES),
        name="gated_delta_rule",
    )(gq, gk, gv, sgg, small, small_t, out_gain.reshape(1, GDN_HEAD_DIM).astype(F32))


def kernel(x, ffn1_norm, ffn1_w_in, ffn1_w_out, mix_norm, w_in, fox_q_norm, fox_k_norm, fox_f_bias,
           gdn_conv, gdn_a_log, gdn_dt_bias, gdn_out_norm, w_out, ffn2_norm, ffn2_w_in, ffn2_w_out):
    b, s, d = x.shape
    depth = ffn1_norm.shape[0]
    h = x.reshape(b * s, d)
    for l in range(depth):
        h, hn = _ffn_call(h, ffn1_norm[l], ffn1_w_in[l], ffn1_w_out[l], next_gain=mix_norm[l])
        qp, kp, vt, gq, gk, gv, sgg, small, small_t = _proj_call(
            hn.reshape(b, s, d), w_in[l], fox_q_norm[l], fox_k_norm[l], fox_f_bias[l],
            gdn_conv[l], gdn_a_log[l], gdn_dt_bias[l])
        y_fox = _attn_call(qp, kp, vt)
        y_gdn = _gdn_call(gq, gk, gv, sgg, small, small_t, gdn_out_norm[l])
        wo = w_out[l].astype(BF16)
        h = _ffn_call(h, ffn2_norm[l], ffn2_w_in[l], ffn2_w_out[l],
                      mix=(y_fox.reshape(b * s, FOX_WIDTH), y_gdn.reshape(b * s, GDN_WIDTH),
                           wo[:FOX_WIDTH], wo[FOX_WIDTH:]))
    return h.reshape(b, s, d)
```

```python
import functools

import jax
import jax.numpy as jnp
from jax import lax
from jax.experimental import pallas as pl
from jax.experimental.pallas import tpu as pltpu

F32 = jnp.float32
BF16 = jnp.bfloat16

D_MODEL = 1024
FOX_HEADS = 8
FOX_HEAD_DIM = 64
FOX_WIDTH = FOX_HEADS * FOX_HEAD_DIM
GDN_HEADS = 4
GDN_HEAD_DIM = 128
GDN_WIDTH = GDN_HEADS * GDN_HEAD_DIM
CONV_WIDTH = 4
D_FF = 2816
EPS = 1e-6

LANES = 128
SUBLANES = 8
VMEM_LIMIT_BYTES = 56 * 1024 * 1024

FOX_PAD = FOX_HEADS * LANES
OFF_Q = 0
OFF_K = OFF_Q + FOX_PAD
OFF_V = OFF_K + FOX_PAD
OFF_G = OFF_V + FOX_WIDTH
OFF_GG = OFF_G + 3 * GDN_WIDTH
OFF_SMALL = OFF_GG + GDN_WIDTH
N_PROJ = OFF_SMALL + LANES

LANE_CUM = 0
LANE_GC = FOX_HEADS
LANE_BETA = LANE_GC + GDN_HEADS
GATE_ROWS = 16
PIECE_LANES = GATE_ROWS

BIAS_LANE = FOX_HEAD_DIM
LOG2E = 1.4426950408889634
BF16_SUBLANES = 16
VT_ROWS = FOX_HEAD_DIM + BF16_SUBLANES

GDN_CHUNK = LANES
NEUMANN_STEPS = 6
GDN_CHUNKS_PER_STEP = 8

FFN_SUBTILE_M = 512
FFN_TILE_M = 2 * FFN_SUBTILE_M
FFN_TILE_F = 256
ATTN_TILE = 512
PROJ_TILE_M = ATTN_TILE
PROJ_SUBTILES = 2
ATTN_HEADS_PER_STEP = 8


def _dot(a, b):
    return jnp.dot(a, b, preferred_element_type=F32)


def _keep_bf16_bits(v):
    bits = lax.bitcast_convert_type(v, jnp.uint32) & jnp.uint32(0xFFFF0000)
    return lax.bitcast_convert_type(bits, F32)


def _split3(v):
    hi = _keep_bf16_bits(v)
    r1 = v - hi
    mid = _keep_bf16_bits(r1)
    lo = _keep_bf16_bits(r1 - mid)
    return hi, mid, lo


def _pack3(v, lane, one_lane=False):
    hi, mid, lo = _split3(v)
    tail = jnp.where(lane == 3 * PIECE_LANES, 1.0, 0.0) if one_lane else 0.0
    packed = jnp.where(lane < PIECE_LANES, hi,
                       jnp.where(lane < 2 * PIECE_LANES, pltpu.roll(mid, PIECE_LANES, 1),
                                 jnp.where(lane < 3 * PIECE_LANES, pltpu.roll(lo, 2 * PIECE_LANES, 1), tail)))
    return packed.astype(BF16)


def _rms_norm(x, gain_row):
    ms = jnp.mean(x * x, axis=-1, keepdims=True)
    return x * lax.rsqrt(ms + EPS) * gain_row


def _silu(x):
    return x / (1.0 + jnp.exp(-x))


def _const_spec(shape):
    nd = len(shape)
    return pl.BlockSpec(shape, lambda *_: (0,) * nd, pipeline_mode=pl.Buffered(1))


def _ffn_body(*refs, n_chunks, tile_f, has_mix):
    if has_mix:
        x_ref, yf_ref, yg_ref, wof_ref, wog_ref, g_ref, win_ref, wout_ref, o_ref = refs
    else:
        x_ref, g_ref, win_ref, wout_ref, gnext_ref, o_ref, xn_next_ref = refs
    for r in range(x_ref.shape[0] // FFN_SUBTILE_M):
        rows = slice(r * FFN_SUBTILE_M, (r + 1) * FFN_SUBTILE_M)
        x = x_ref[rows, :]
        if has_mix:
            x = x + _dot(yf_ref[rows, :], wof_ref[...]) + _dot(yg_ref[rows, :], wog_ref[...])
        xn = _rms_norm(x, g_ref[...]).astype(BF16)
        acc = None
        for c in range(n_chunks):
            cols = slice(c * tile_f, (c + 1) * tile_f)
            gate = _dot(xn, win_ref[:, cols])
            up = _dot(xn, win_ref[:, D_FF + c * tile_f:D_FF + (c + 1) * tile_f])
            act = (_silu(gate) * up).astype(BF16)
            part = _dot(act, wout_ref[cols, :])
            acc = part if acc is None else acc + part
        out = x + 0.5 * acc
        o_ref[rows, :] = out
        if not has_mix:
            xn_next_ref[rows, :] = _rms_norm(out, gnext_ref[...]).astype(BF16)


def _ffn_call(x2d, gain, w_in, w_out, mix=None, next_gain=None):
    m, d = x2d.shape
    tm = FFN_TILE_M if m % FFN_TILE_M == 0 else FFN_SUBTILE_M
    assert m % tm == 0, (m, tm)
    tf = FFN_TILE_F
    nc = D_FF // tf
    w_in_r = w_in.astype(BF16)
    w_out_r = w_out.astype(BF16)
    row = pl.BlockSpec((tm, d), lambda i: (i, 0))
    in_specs = [row]
    args = [x2d]
    if mix is not None:
        y_fox, y_gdn, wo_f, wo_g = mix
        in_specs += [pl.BlockSpec((tm, FOX_WIDTH), lambda i: (i, 0)),
                     pl.BlockSpec((tm, GDN_WIDTH), lambda i: (i, 0)),
                     _const_spec(wo_f.shape), _const_spec(wo_g.shape)]
        args += [y_fox, y_gdn, wo_f, wo_g]
    in_specs += [_const_spec((1, d)), _const_spec(w_in_r.shape), _const_spec(w_out_r.shape)]
    args += [gain.reshape(1, d), w_in_r, w_out_r]
    out_specs = row
    out_shape = jax.ShapeDtypeStruct((m, d), F32)
    if mix is None:
        in_specs += [_const_spec((1, d))]
        args += [next_gain.reshape(1, d)]
        out_specs = [row, row]
        out_shape = [out_shape, jax.ShapeDtypeStruct((m, d), BF16)]
    return pl.pallas_call(
        functools.partial(_ffn_body, n_chunks=nc, tile_f=tf, has_mix=mix is not None),
        grid=(m // tm,),
        in_specs=in_specs,
        out_specs=out_specs,
        out_shape=out_shape,
        compiler_params=pltpu.CompilerParams(dimension_semantics=("arbitrary",),
                                             vmem_limit_bytes=VMEM_LIMIT_BYTES),
        name="ffn_mix" if mix is not None else "ffn",
    )(*args)


def _proj_body(xn_ref, w_ref, cw_ref, qg_ref, browt_ref, alogt_ref, triu_ref, pmat_ref,
               qp_ref, kp_ref, vt_ref, gq_ref, gk_ref, gv_ref, sgg_ref, small_ref, smallt_ref,
               cum_carry, conv_a, conv_b, pack_buf, *, tm, n_sub):
    @pl.when(pl.program_id(1) == 0)
    def _():
        cum_carry[...] = jnp.zeros_like(cum_carry)
        conv_a[0:SUBLANES, :] = jnp.zeros((SUBLANES, 3 * GDN_WIDTH), F32)

    conv_bufs = (conv_a, conv_b)
    for sub in range(n_sub):
        _proj_tile(sub, slice(sub * tm, (sub + 1) * tm), conv_bufs[sub], conv_bufs[(sub + 1) % n_sub],
                   xn_ref, w_ref, cw_ref, qg_ref, browt_ref, alogt_ref, triu_ref, pmat_ref,
                   qp_ref, kp_ref, vt_ref, gq_ref, gk_ref, gv_ref, sgg_ref, small_ref, smallt_ref,
                   cum_carry, pack_buf, tm)


def _proj_tile(sub, rows, conv_buf, conv_next, xn_ref, w_ref, cw_ref, qg_ref, browt_ref, alogt_ref, triu_ref,
               pmat_ref, qp_ref, kp_ref, vt_ref, gq_ref, gk_ref, gv_ref, sgg_ref, small_ref, smallt_ref,
               cum_carry, pack_buf, tm):
    xn = xn_ref[0, rows, :]
    lane = lax.broadcasted_iota(jnp.int32, (tm, LANES), 1)

    z = _dot(xn, w_ref[:, OFF_SMALL:OFF_SMALL + LANES])
    pg = _dot(xn, w_ref[:, OFF_G:OFF_G + 3 * GDN_WIDTH])

    zt = z.T[0:GATE_ROWS] + browt_ref[...]
    grow = lax.broadcasted_iota(jnp.int32, (GATE_ROWS, tm), 0)
    e = jnp.log1p(jnp.exp(-jnp.abs(zt)))
    log_sig = jnp.minimum(zt, 0.0) - e
    softplus = jnp.maximum(zt, 0.0) + e
    sigmoid = 1.0 / (1.0 + jnp.exp(-zt))
    log_decay = -jnp.exp(alogt_ref[...]) * softplus
    pieces = jnp.concatenate(_split3(jnp.where(grow < LANE_GC, log_sig, log_decay)), axis=0).astype(BF16)
    c3 = _dot(pieces, triu_ref[...])
    c = c3[0:GATE_ROWS] + c3[GATE_ROWS:2 * GATE_ROWS] + c3[2 * GATE_ROWS:]
    cum_t = c[:, :tm] + jnp.concatenate([cum_carry[...]] * (tm // LANES), axis=1)
    pq = _dot(xn, w_ref[:, OFF_Q:OFF_Q + FOX_PAD])
    cum_carry[...] = jnp.broadcast_to(cum_t[:, tm - 1:tm], (GATE_ROWS, LANES))
    small_t = jnp.where(grow < LANE_GC, cum_t, jnp.where(grow < LANE_BETA, c[:, tm:], sigmoid))
    smallt_ref[0, :, rows] = small_t
    small = jnp.concatenate([small_t, jnp.zeros((LANES - GATE_ROWS, tm), F32)], axis=0).T
    small_ref[0, rows, :] = small

    pack_buf[sub] = _pack3(small * LOG2E, lane, one_lane=True)
    bias = _dot(pack_buf[sub], pmat_ref[...])
    pk = _dot(xn, w_ref[:, OFF_K:OFF_K + FOX_PAD])
    pv = _dot(xn, w_ref[:, OFF_V:OFF_V + FOX_WIDTH])
    pgg = _dot(xn, w_ref[:, OFF_GG:OFF_GG + GDN_WIDTH])
    for off, p, out_ref in ((OFF_Q, pq, qp_ref), (OFF_K, pk, kp_ref)):
        for h in range(FOX_HEADS):
            blk = p[:, h * LANES:(h + 1) * LANES]
            ms = jnp.sum(blk * blk, axis=-1, keepdims=True) * (1.0 / FOX_HEAD_DIM)
            nrm = blk * lax.rsqrt(ms + EPS)
            if off == OFF_Q:
                full = nrm * qg_ref[...] + bias[:, off + h * LANES:off + (h + 1) * LANES]
                out_ref[0, sub, h * LANES:(h + 1) * LANES, :] = full.T.astype(BF16)
            else:
                out_ref[0, rows, h * LANES:(h + 1) * LANES] = (
                    nrm + bias[:, off + h * LANES:off + (h + 1) * LANES]).astype(BF16)

    v_t = pv.T.astype(BF16)
    for h in range(FOX_HEADS):
        vt_ref[0, sub, h * VT_ROWS:h * VT_ROWS + FOX_HEAD_DIM, :] = v_t[h * FOX_HEAD_DIM:(h + 1) * FOX_HEAD_DIM, :]
        vt_ref[0, sub, h * VT_ROWS + FOX_HEAD_DIM:(h + 1) * VT_ROWS, :] = jnp.ones((VT_ROWS - FOX_HEAD_DIM, tm), BF16)

    conv_buf[SUBLANES:SUBLANES + tm, :] = pg
    ext = conv_buf[...]
    y = cw_ref[CONV_WIDTH - 1:CONV_WIDTH, :] * pg
    for back in range(1, CONV_WIDTH):
        k = CONV_WIDTH - 1 - back
        y = y + cw_ref[k:k + 1, :] * pltpu.roll(ext, back, 0)[SUBLANES:SUBLANES + tm, :]
    conv_next[0:SUBLANES, :] = conv_buf[tm:tm + SUBLANES, :]
    y = _silu(y)
    for idx, out_ref in enumerate((gq_ref, gk_ref, gv_ref)):
        for h in range(GDN_HEADS):
            lo = idx * GDN_WIDTH + h * GDN_HEAD_DIM
            blk = y[:, lo:lo + GDN_HEAD_DIM]
            if idx < 2:
                blk = blk * lax.rsqrt(jnp.sum(blk * blk, axis=-1, keepdims=True) + EPS)
            if idx == 0:
                blk = blk * (GDN_HEAD_DIM ** -0.5)
            out_ref[0, rows, h * GDN_HEAD_DIM:(h + 1) * GDN_HEAD_DIM] = blk.astype(BF16)

    sgg_ref[0, rows, :] = _silu(pgg).astype(BF16)


def _proj_weights(w_in, fox_q_norm, fox_k_norm, fox_f_bias, gdn_a_log, gdn_dt_bias):
    d = w_in.shape[0]
    fw = FOX_WIDTH
    fq, fk, fv = w_in[:, 0:fw], w_in[:, fw:2 * fw], w_in[:, 2 * fw:3 * fw]
    o = 3 * fw
    ff = w_in[:, o:o + FOX_HEADS]
    o += FOX_HEADS
    gqkv = w_in[:, o:o + 3 * GDN_WIDTH]
    o += 3 * GDN_WIDTH
    ga = w_in[:, o:o + GDN_HEADS]
    gb = w_in[:, o + GDN_HEADS:o + 2 * GDN_HEADS]
    gg = w_in[:, o + 2 * GDN_HEADS:]

    def pad_heads(w):
        w = w.reshape(d, FOX_HEADS, FOX_HEAD_DIM)
        return jnp.pad(w, ((0, 0), (0, 0), (0, LANES - FOX_HEAD_DIM))).reshape(d, FOX_PAD)

    small = jnp.pad(jnp.concatenate([ff, ga, gb], axis=1), ((0, 0), (0, LANES - FOX_HEADS - 2 * GDN_HEADS)))
    w = jnp.concatenate([pad_heads(fq), pad_heads(fk), fv, gqkv, gg, small], axis=1).astype(BF16)

    def lane_row(vals, start):
        return jnp.zeros((1, LANES), F32).at[0, start:start + vals.shape[0]].set(vals.astype(F32))

    qg = lane_row(fox_q_norm * fox_k_norm * (FOX_HEAD_DIM ** -0.5 * LOG2E), 0)
    brow = lane_row(fox_f_bias, LANE_CUM) + lane_row(gdn_dt_bias, LANE_GC)
    alog = lane_row(gdn_a_log, LANE_GC)

    pm = jnp.zeros((LANES, 2, FOX_HEADS, LANES), F32)
    hh = jnp.arange(FOX_HEADS)
    for piece in range(3):
        src = piece * PIECE_LANES + LANE_CUM + hh
        pm = pm.at[src, 0, hh, BIAS_LANE + piece].set(1.0)
        pm = pm.at[src, 1, hh, BIAS_LANE + 3 + piece].set(-1.0)
        pm = pm.at[3 * PIECE_LANES, 0, hh, BIAS_LANE + 3 + piece].set(1.0)
        pm = pm.at[3 * PIECE_LANES, 1, hh, BIAS_LANE + piece].set(1.0)
    pmat = pm.reshape(LANES, 2 * FOX_PAD).astype(BF16)
    return w, qg, brow, alog, pmat


def _proj_call(x3d, w_in, fox_q_norm, fox_k_norm, fox_f_bias, gdn_conv, gdn_a_log, gdn_dt_bias):
    b, s, d = x3d.shape
    tm = min(PROJ_TILE_M, s)
    w, qg, brow, alog, pmat = _proj_weights(w_in, fox_q_norm, fox_k_norm, fox_f_bias, gdn_a_log, gdn_dt_bias)
    r = jnp.arange(tm)
    upper = r[:, None] <= r[None, :]
    same_chunk = (r[:, None] // GDN_CHUNK) == (r[None, :] // GDN_CHUNK)
    triu = jnp.concatenate([upper, upper & same_chunk], axis=1).astype(BF16)
    brow_t = jnp.broadcast_to(brow[0, :GATE_ROWS, None], (GATE_ROWS, tm))
    alog_t = jnp.broadcast_to(alog[0, :GATE_ROWS, None], (GATE_ROWS, tm))

    n_sub = PROJ_SUBTILES if s % (PROJ_SUBTILES * tm) == 0 else 1
    ts = n_sub * tm

    def tok(width, dtype):
        return (pl.BlockSpec((1, ts, width), lambda bi, si: (bi, si, 0)),
                jax.ShapeDtypeStruct((b, s, width), dtype))

    outs = [(pl.BlockSpec((1, n_sub, FOX_PAD, tm), lambda bi, si: (bi, si, 0, 0)),
             jax.ShapeDtypeStruct((b, s // tm, FOX_PAD, tm), BF16)),
            tok(FOX_PAD, BF16),
            (pl.BlockSpec((1, n_sub, FOX_HEADS * VT_ROWS, tm), lambda bi, si: (bi, si, 0, 0)),
             jax.ShapeDtypeStruct((b, s // tm, FOX_HEADS * VT_ROWS, tm), BF16)),
            tok(GDN_WIDTH, BF16), tok(GDN_WIDTH, BF16), tok(GDN_WIDTH, BF16),
            tok(GDN_WIDTH, BF16), tok(LANES, F32),
            (pl.BlockSpec((1, GATE_ROWS, ts), lambda bi, si: (bi, 0, si)),
             jax.ShapeDtypeStruct((b, GATE_ROWS, s), F32))]
    consts = [w, gdn_conv.astype(F32), qg, brow_t, alog_t, triu, pmat]
    conv_scratch = pltpu.VMEM((tm + SUBLANES, 3 * GDN_WIDTH), F32)
    return pl.pallas_call(
        functools.partial(_proj_body, tm=tm, n_sub=n_sub),
        grid=(b, s // ts),
        in_specs=[pl.BlockSpec((1, ts, d), lambda bi, si: (bi, si, 0))] + [_const_spec(c.shape) for c in consts],
        out_specs=[o[0] for o in outs],
        out_shape=[o[1] for o in outs],
        scratch_shapes=[pltpu.VMEM((GATE_ROWS, LANES), F32), conv_scratch, conv_scratch,
                        pltpu.VMEM((n_sub, tm, LANES), BF16)],
        compiler_params=pltpu.CompilerParams(dimension_semantics=("arbitrary", "arbitrary"),
                                             vmem_limit_bytes=VMEM_LIMIT_BYTES),
        name="mixer_proj",
    )(x3d, *consts)


def _attn_body(q_ref, k_ref, vt_ref, o_ref, *, tile):
    i = pl.program_id(2)
    nh = ATTN_HEADS_PER_STEP
    dh = FOX_HEAD_DIM

    def logits(hh, j):
        off = pl.multiple_of(j * tile, tile)
        q_t = q_ref[0, 0, hh * LANES:(hh + 1) * LANES, :]
        k = k_ref[0, pl.ds(off, tile), hh * LANES:(hh + 1) * LANES]
        return _dot(k, q_t)

    def step(j, carry, diagonal):
        ahead = 2
        s_all = {hh: logits(hh, j) for hh in range(min(ahead, nh))}
        out = []
        for hh in range(nh):
            if hh + ahead < nh:
                s_all[hh + ahead] = logits(hh + ahead, j)
            m, acc = carry[hh]
            s = s_all.pop(hh)
            if diagonal:
                key = lax.broadcasted_iota(jnp.int32, s.shape, 0)
                qry = lax.broadcasted_iota(jnp.int32, s.shape, 1)
                s = jnp.where(qry >= key, s, -jnp.inf)
            m_new = jnp.maximum(m, jnp.max(s, axis=0, keepdims=True))
            p = jnp.exp2(s - m_new).astype(BF16)
            vt = vt_ref[0, j, hh * VT_ROWS:(hh + 1) * VT_ROWS, :]
            acc = jnp.exp2(m - m_new) * acc + _dot(vt, p)
            out.append((m_new, acc))
        return tuple(out)

    init = tuple((jnp.full((1, tile), -jnp.inf, F32), jnp.zeros((VT_ROWS, tile), F32)) for _ in range(nh))
    carry = lax.fori_loop(0, i, lambda j, c: step(j, c, False), init)
    carry = step(i, carry, True)
    out_t = jnp.concatenate([acc[:dh] / acc[dh:dh + 1] for _, acc in carry], axis=0)
    o_ref[0] = out_t.T.astype(BF16)


def _attn_call(qp, kp, vt):
    b, s, _ = kp.shape
    tile = min(ATTN_TILE, s)
    nh = ATTN_HEADS_PER_STEP
    return pl.pallas_call(
        functools.partial(_attn_body, tile=tile),
        grid=(b, FOX_HEADS // nh, s // tile),
        in_specs=[pl.BlockSpec((1, 1, nh * LANES, tile), lambda bi, hi, qi: (bi, qi, hi, 0)),
                  pl.BlockSpec((1, s, nh * LANES), lambda bi, hi, qi: (bi, 0, hi)),
                  pl.BlockSpec((1, s // tile, nh * VT_ROWS, tile), lambda bi, hi, qi: (bi, 0, hi, 0))],
        out_specs=pl.BlockSpec((1, tile, nh * FOX_HEAD_DIM), lambda bi, hi, qi: (bi, qi, hi)),
        out_shape=jax.ShapeDtypeStruct((b, s, FOX_WIDTH), BF16),
        compiler_params=pltpu.CompilerParams(dimension_semantics=("arbitrary", "arbitrary", "arbitrary"),
                                             vmem_limit_bytes=VMEM_LIMIT_BYTES),
        name="fox_attention",
    )(qp, kp, vt)


def _gdn_body(q_ref, k_ref, v_ref, sgg_ref, small_ref, smallt_ref, gain_ref, y_ref, state_ref, *, n_chunks):
    @pl.when(pl.program_id(1) == 0)
    def _():
        state_ref[...] = jnp.zeros_like(state_ref)

    n = GDN_CHUNK
    row = lax.broadcasted_iota(jnp.int32, (n, n), 0)
    col = lax.broadcasted_iota(jnp.int32, (n, n), 1)
    incl = row >= col
    strict = row > col
    eye = (row == col).astype(F32)

    pairs = [(c, h) for c in range(n_chunks) for h in range(GDN_HEADS)]
    pre = []
    for c, h in pairs:
        rows = slice(c * n, (c + 1) * n)
        hs = slice(h * GDN_HEAD_DIM, (h + 1) * GDN_HEAD_DIM)
        sm = small_ref[0, rows, :]
        smt = smallt_ref[0, :, rows]
        gcc = jnp.sum(jnp.where(col == LANE_GC + h, sm, 0.0), axis=1, keepdims=True)
        beta = jnp.sum(jnp.where(col == LANE_BETA + h, sm, 0.0), axis=1, keepdims=True)
        gcr = jnp.sum(jnp.where(row[:GATE_ROWS] == LANE_GC + h, smt, 0.0), axis=0, keepdims=True)
        g_last = jnp.sum(jnp.where(col[0:1, :] == n - 1, gcr, 0.0), axis=1, keepdims=True)
        k_b = k_ref[0, rows, hs]
        k = k_b.astype(F32)
        kb = k * beta
        k_t = k.T
        pre.append(dict(rows=rows, hs=hs, gcc=gcc, g_last=g_last, kb=kb, k_tb=k_t.astype(BF16),
                        kd_t=(k_t * jnp.exp(g_last - gcr)).astype(BF16),
                        q_b=q_ref[0, rows, hs], vb=v_ref[0, rows, hs].astype(F32) * beta,
                        decay=jnp.exp(jnp.where(incl, gcc - gcr, -jnp.inf)), e_g=jnp.exp(gcc)))

    x_pow = [jnp.where(strict, -(_dot(p["kb"].astype(BF16), p["k_tb"]) * p["decay"]), 0.0) for p in pre]
    t = [eye + x for x in x_pow]
    x_pow = [_dot(x.astype(BF16), x.astype(BF16)) for x in x_pow]
    for step in range(1, NEUMANN_STEPS + 1):
        if step < NEUMANN_STEPS:
            x_b = [x.astype(BF16) for x in x_pow]
            both = [_dot(jnp.concatenate([xb, ti.astype(BF16)], axis=0), xb) for xb, ti in zip(x_b, t)]
            x_pow = [r[:n] for r in both]
            t = [ti + r[n:] for ti, r in zip(t, both)]
        else:
            t = [ti + _dot(ti.astype(BF16), x.astype(BF16)) for x, ti in zip(x_pow, t)]

    uw = [_dot(ti.astype(BF16), jnp.concatenate([p["vb"], p["kb"] * p["e_g"]], axis=1).astype(BF16))
          for ti, p in zip(t, pre)]
    attn = [jnp.where(incl, _dot(p["q_b"], p["k_tb"]) * p["decay"], 0.0).astype(BF16) for p in pre]

    states = [state_ref[h] for h in range(GDN_HEADS)]
    for c in range(n_chunks):
        idx = [c * GDN_HEADS + h for h in range(GDN_HEADS)]
        st_b = [st.astype(BF16) for st in states]
        v_new = [(uw[i][:, :GDN_HEAD_DIM] - _dot(uw[i][:, GDN_HEAD_DIM:].astype(BF16), st_b[h])).astype(BF16)
                 for h, i in enumerate(idx)]
        new_states = [states[h] * jnp.exp(pre[i]["g_last"]) + _dot(pre[i]["kd_t"], v_new[h])
                      for h, i in enumerate(idx)]
        for h, i in enumerate(idx):
            p = pre[i]
            qd = (p["q_b"].astype(F32) * p["e_g"]).astype(BF16)
            o = _dot(jnp.concatenate([qd, attn[i]], axis=1), jnp.concatenate([st_b[h], v_new[h]], axis=0))
            y = _rms_norm(o, gain_ref[...]) * sgg_ref[0, p["rows"], p["hs"]].astype(F32)
            y_ref[0, p["rows"], p["hs"]] = y.astype(BF16)
        states = new_states

    for h in range(GDN_HEADS):
        state_ref[h] = states[h]


def _gdn_call(gq, gk, gv, sgg, small, small_t, out_gain):
    b, s, _ = gq.shape
    n_chunks = min(GDN_CHUNKS_PER_STEP, s // GDN_CHUNK)
    ts = n_chunks * GDN_CHUNK
    tok = pl.BlockSpec((1, ts, GDN_WIDTH), lambda bi, ci: (bi, ci, 0))
    return pl.pallas_call(
        functools.partial(_gdn_body, n_chunks=n_chunks),
        grid=(b, s // ts),
        in_specs=[tok, tok, tok, tok,
                  pl.BlockSpec((1, ts, LANES), lambda bi, ci: (bi, ci, 0)),
                  pl.BlockSpec((1, GATE_ROWS, ts), lambda bi, ci: (bi, 0, ci)),
                  _const_spec((1, GDN_HEAD_DIM))],
        out_specs=tok,
        out_shape=jax.ShapeDtypeStruct((b, s, GDN_WIDTH), BF16),
        scratch_shapes=[pltpu.VMEM((GDN_HEADS, GDN_HEAD_DIM, GDN_HEAD_DIM), F32)],
        compiler_params=pltpu.CompilerParams(dimension_semantics=("arbitrary", "arbitrary"),
                                             vmem_limit_bytes=VMEM_LIMIT_BYTES),
        name="gated_delta_rule",
    )(gq, gk, gv, sgg, small, small_t, out_gain.reshape(1, GDN_HEAD_DIM).astype(F32))


def kernel(x, ffn1_norm, ffn1_w_in, ffn1_w_out, mix_norm, w_in, fox_q_norm, fox_k_norm, fox_f_bias,
           gdn_conv, gdn_a_log, gdn_dt_bias, gdn_out_norm, w_out, ffn2_norm, ffn2_w_in, ffn2_w_out):
    b, s, d = x.shape
    depth = ffn1_norm.shape[0]
    h = x.reshape(b * s, d)
    for l in range(depth):
        h, hn = _ffn_call(h, ffn1_norm[l], ffn1_w_in[l], ffn1_w_out[l], next_gain=mix_norm[l])
        qp, kp, vt, gq, gk, gv, sgg, small, small_t = _proj_call(
            hn.reshape(b, s, d), w_in[l], fox_q_norm[l], fox_k_norm[l], fox_f_bias[l],
            gdn_conv[l], gdn_a_log[l], gdn_dt_bias[l])
        y_fox = _attn_call(qp, kp, vt)
        y_gdn = _gdn_call(gq, gk, gv, sgg, small, small_t, gdn_out_norm[l])
        wo = w_out[l].astype(BF16)
        h = _ffn_call(h, ffn2_norm[l], ffn2_w_in[l], ffn2_w_out[l],
                      mix=(y_fox.reshape(b * s, FOX_WIDTH), y_gdn.reshape(b * s, GDN_WIDTH),
                           wo[:FOX_WIDTH], wo[FOX_WIDTH:]))
    return h.reshape(b, s, d)
```

```python
import functools

import jax
import jax.numpy as jnp
from jax import lax
from jax.experimental import pallas as pl
from jax.experimental.pallas import tpu as pltpu

F32 = jnp.float32
BF16 = jnp.bfloat16

D_MODEL = 1024
FOX_HEADS = 8
FOX_HEAD_DIM = 64
FOX_WIDTH = FOX_HEADS * FOX_HEAD_DIM
GDN_HEADS = 4
GDN_HEAD_DIM = 128
GDN_WIDTH = GDN_HEADS * GDN_HEAD_DIM
CONV_WIDTH = 4
D_FF = 2816
EPS = 1e-6

LANES = 128
SUBLANES = 8
VMEM_LIMIT_BYTES = 56 * 1024 * 1024

FOX_PAD = FOX_HEADS * LANES
OFF_Q = 0
OFF_K = OFF_Q + FOX_PAD
OFF_V = OFF_K + FOX_PAD
OFF_G = OFF_V + FOX_WIDTH
OFF_GG = OFF_G + 3 * GDN_WIDTH
OFF_SMALL = OFF_GG + GDN_WIDTH
N_PROJ = OFF_SMALL + LANES

LANE_CUM = 0
LANE_GC = FOX_HEADS
LANE_BETA = LANE_GC + GDN_HEADS
GATE_ROWS = 16
PIECE_LANES = GATE_ROWS

BIAS_LANE = FOX_HEAD_DIM
LOG2E = 1.4426950408889634
BF16_SUBLANES = 16
VT_ROWS = FOX_HEAD_DIM + BF16_SUBLANES

GDN_CHUNK = LANES
NEUMANN_STEPS = 6
GDN_CHUNKS_PER_STEP = 8

FFN_SUBTILE_M = 512
FFN_TILE_M = 2 * FFN_SUBTILE_M
FFN_TILE_F = 256
ATTN_TILE = 512
PROJ_TILE_M = ATTN_TILE
PROJ_SUBTILES = 2
ATTN_HEADS_PER_STEP = 8


def _dot(a, b):
    return jnp.dot(a, b, preferred_element_type=F32)


def _keep_bf16_bits(v):
    bits = lax.bitcast_convert_type(v, jnp.uint32) & jnp.uint32(0xFFFF0000)
    return lax.bitcast_convert_type(bits, F32)


def _split3(v):
    hi = _keep_bf16_bits(v)
    r1 = v - hi
    mid = _keep_bf16_bits(r1)
    lo = _keep_bf16_bits(r1 - mid)
    return hi, mid, lo


def _pack3(v, lane, one_lane=False):
    hi, mid, lo = _split3(v)
    tail = jnp.where(lane == 3 * PIECE_LANES, 1.0, 0.0) if one_lane else 0.0
    packed = jnp.where(lane < PIECE_LANES, hi,
                       jnp.where(lane < 2 * PIECE_LANES, pltpu.roll(mid, PIECE_LANES, 1),
                                 jnp.where(lane < 3 * PIECE_LANES, pltpu.roll(lo, 2 * PIECE_LANES, 1), tail)))
    return packed.astype(BF16)


def _rms_norm(x, gain_row):
    ms = jnp.mean(x * x, axis=-1, keepdims=True)
    return x * lax.rsqrt(ms + EPS) * gain_row


def _silu(x):
    return x / (1.0 + jnp.exp(-x))


def _const_spec(shape):
    nd = len(shape)
    return pl.BlockSpec(shape, lambda *_: (0,) * nd, pipeline_mode=pl.Buffered(1))


def _ffn_body(*refs, n_chunks, tile_f, has_mix):
    if has_mix:
        x_ref, yf_ref, yg_ref, wof_ref, wog_ref, g_ref, win_ref, wout_ref, o_ref = refs
    else:
        x_ref, g_ref, win_ref, wout_ref, gnext_ref, o_ref, xn_next_ref = refs
    for r in range(x_ref.shape[0] // FFN_SUBTILE_M):
        rows = slice(r * FFN_SUBTILE_M, (r + 1) * FFN_SUBTILE_M)
        x = x_ref[rows, :]
        if has_mix:
            x = x + _dot(yf_ref[rows, :], wof_ref[...]) + _dot(yg_ref[rows, :], wog_ref[...])
        xn = _rms_norm(x, g_ref[...]).astype(BF16)
        acc = None
        for c in range(n_chunks):
            cols = slice(c * tile_f, (c + 1) * tile_f)
            gate = _dot(xn, win_ref[:, cols])
            up = _dot(xn, win_ref[:, D_FF + c * tile_f:D_FF + (c + 1) * tile_f])
            act = (_silu(gate) * up).astype(BF16)
            part = _dot(act, wout_ref[cols, :])
            acc = part if acc is None else acc + part
        out = x + 0.5 * acc
        o_ref[rows, :] = out
        if not has_mix:
            xn_next_ref[rows, :] = _rms_norm(out, gnext_ref[...]).astype(BF16)


def _ffn_call(x2d, gain, w_in, w_out, mix=None, next_gain=None):
    m, d = x2d.shape
    tm = FFN_TILE_M if m % FFN_TILE_M == 0 else FFN_SUBTILE_M
    assert m % tm == 0, (m, tm)
    tf = FFN_TILE_F
    nc = D_FF // tf
    w_in_r = w_in.astype(BF16)
    w_out_r = w_out.astype(BF16)
    row = pl.BlockSpec((tm, d), lambda i: (i, 0))
    in_specs = [row]
    args = [x2d]
    if mix is not None:
        y_fox, y_gdn, wo_f, wo_g = mix
        in_specs += [pl.BlockSpec((tm, FOX_WIDTH), lambda i: (i, 0)),
                     pl.BlockSpec((tm, GDN_WIDTH), lambda i: (i, 0)),
                     _const_spec(wo_f.shape), _const_spec(wo_g.shape)]
        args += [y_fox, y_gdn, wo_f, wo_g]
    in_specs += [_const_spec((1, d)), _const_spec(w_in_r.shape), _const_spec(w_out_r.shape)]
    args += [gain.reshape(1, d), w_in_r, w_out_r]
    out_specs = row
    out_shape = jax.ShapeDtypeStruct((m, d), F32)
    if mix is None:
        in_specs += [_const_spec((1, d))]
        args += [next_gain.reshape(1, d)]
        out_specs = [row, row]
        out_shape = [out_shape, jax.ShapeDtypeStruct((m, d), BF16)]
    return pl.pallas_call(
        functools.partial(_ffn_body, n_chunks=nc, tile_f=tf, has_mix=mix is not None),
        grid=(m // tm,),
        in_specs=in_specs,
        out_specs=out_specs,
        out_shape=out_shape,
        compiler_params=pltpu.CompilerParams(dimension_semantics=("arbitrary",),
                                             vmem_limit_bytes=VMEM_LIMIT_BYTES),
        name="ffn_mix" if mix is not None else "ffn",
    )(*args)


def _proj_body(xn_ref, w_ref, cw_ref, qg_ref, browt_ref, alogt_ref, triu_ref, pmat_ref,
               qp_ref, kp_ref, vt_ref, gq_ref, gk_ref, gv_ref, sgg_ref, small_ref, smallt_ref,
               cum_carry, conv_a, conv_b, pack_buf, *, tm, n_sub):
    @pl.when(pl.program_id(1) == 0)
    def _():
        cum_carry[...] = jnp.zeros_like(cum_carry)
        conv_a[0:SUBLANES, :] = jnp.zeros((SUBLANES, 3 * GDN_WIDTH), F32)

    conv_bufs = (conv_a, conv_b)
    for sub in range(n_sub):
        _proj_tile(sub, slice(sub * tm, (sub + 1) * tm), conv_bufs[sub], conv_bufs[(sub + 1) % n_sub],
                   xn_ref, w_ref, cw_ref, qg_ref, browt_ref, alogt_ref, triu_ref, pmat_ref,
                   qp_ref, kp_ref, vt_ref, gq_ref, gk_ref, gv_ref, sgg_ref, small_ref, smallt_ref,
                   cum_carry, pack_buf, tm)


def _proj_tile(sub, rows, conv_buf, conv_next, xn_ref, w_ref, cw_ref, qg_ref, browt_ref, alogt_ref, triu_ref,
               pmat_ref, qp_ref, kp_ref, vt_ref, gq_ref, gk_ref, gv_ref, sgg_ref, small_ref, smallt_ref,
               cum_carry, pack_buf, tm):
    xn = xn_ref[0, rows, :]
    lane = lax.broadcasted_iota(jnp.int32, (tm, LANES), 1)

    z = _dot(xn, w_ref[:, OFF_SMALL:OFF_SMALL + LANES])
    pg = _dot(xn, w_ref[:, OFF_G:OFF_G + 3 * GDN_WIDTH])

    zt = z.T[0:GATE_ROWS] + browt_ref[...]
    grow = lax.broadcasted_iota(jnp.int32, (GATE_ROWS, tm), 0)
    e = jnp.log1p(jnp.exp(-jnp.abs(zt)))
    log_sig = jnp.minimum(zt, 0.0) - e
    softplus = jnp.maximum(zt, 0.0) + e
    sigmoid = 1.0 / (1.0 + jnp.exp(-zt))
    log_decay = -jnp.exp(alogt_ref[...]) * softplus
    pieces = jnp.concatenate(_split3(jnp.where(grow < LANE_GC, log_sig, log_decay)), axis=0).astype(BF16)
    c3 = _dot(pieces, triu_ref[...])
    c = c3[0:GATE_ROWS] + c3[GATE_ROWS:2 * GATE_ROWS] + c3[2 * GATE_ROWS:]
    cum_t = c[:, :tm] + jnp.concatenate([cum_carry[...]] * (tm // LANES), axis=1)
    pq = _dot(xn, w_ref[:, OFF_Q:OFF_Q + FOX_PAD])
    cum_carry[...] = jnp.broadcast_to(cum_t[:, tm - 1:tm], (GATE_ROWS, LANES))
    small_t = jnp.where(grow < LANE_GC, cum_t, jnp.where(grow < LANE_BETA, c[:, tm:], sigmoid))
    smallt_ref[0, :, rows] = small_t
    small = jnp.concatenate([small_t, jnp.zeros((LANES - GATE_ROWS, tm), F32)], axis=0).T
    small_ref[0, rows, :] = small

    pack_buf[sub] = _pack3(small * LOG2E, lane, one_lane=True)
    bias = _dot(pack_buf[sub], pmat_ref[...])
    pk = _dot(xn, w_ref[:, OFF_K:OFF_K + FOX_PAD])
    pv = _dot(xn, w_ref[:, OFF_V:OFF_V + FOX_WIDTH])
    pgg = _dot(xn, w_ref[:, OFF_GG:OFF_GG + GDN_WIDTH])
    for off, p, out_ref in ((OFF_Q, pq, qp_ref), (OFF_K, pk, kp_ref)):
        for h in range(FOX_HEADS):
            blk = p[:, h * LANES:(h + 1) * LANES]
            ms = jnp.sum(blk * blk, axis=-1, keepdims=True) * (1.0 / FOX_HEAD_DIM)
            nrm = blk * lax.rsqrt(ms + EPS)
            if off == OFF_Q:
                full = nrm * qg_ref[...] + bias[:, off + h * LANES:off + (h + 1) * LANES]
                out_ref[0, sub, h * LANES:(h + 1) * LANES, :] = full.astype(BF16).T
            else:
                out_ref[0, rows, h * LANES:(h + 1) * LANES] = (
                    nrm + bias[:, off + h * LANES:off + (h + 1) * LANES]).astype(BF16)

    v_t = pv.T.astype(BF16)
    for h in range(FOX_HEADS):
        vt_ref[0, sub, h * VT_ROWS:h * VT_ROWS + FOX_HEAD_DIM, :] = v_t[h * FOX_HEAD_DIM:(h + 1) * FOX_HEAD_DIM, :]
        vt_ref[0, sub, h * VT_ROWS + FOX_HEAD_DIM:(h + 1) * VT_ROWS, :] = jnp.ones((VT_ROWS - FOX_HEAD_DIM, tm), BF16)

    conv_buf[SUBLANES:SUBLANES + tm, :] = pg
    ext = conv_buf[...]
    y = cw_ref[CONV_WIDTH - 1:CONV_WIDTH, :] * pg
    for back in range(1, CONV_WIDTH):
        k = CONV_WIDTH - 1 - back
        y = y + cw_ref[k:k + 1, :] * pltpu.roll(ext, back, 0)[SUBLANES:SUBLANES + tm, :]
    conv_next[0:SUBLANES, :] = conv_buf[tm:tm + SUBLANES, :]
    y = _silu(y)
    for idx, out_ref in enumerate((gq_ref, gk_ref, gv_ref)):
        for h in range(GDN_HEADS):
            lo = idx * GDN_WIDTH + h * GDN_HEAD_DIM
            blk = y[:, lo:lo + GDN_HEAD_DIM]
            if idx < 2:
                blk = blk * lax.rsqrt(jnp.sum(blk * blk, axis=-1, keepdims=True) + EPS)
            if idx == 0:
                blk = blk * (GDN_HEAD_DIM ** -0.5)
            out_ref[0, rows, h * GDN_HEAD_DIM:(h + 1) * GDN_HEAD_DIM] = blk.astype(BF16)

    sgg_ref[0, rows, :] = _silu(pgg).astype(BF16)


def _proj_weights(w_in, fox_q_norm, fox_k_norm, fox_f_bias, gdn_a_log, gdn_dt_bias):
    d = w_in.shape[0]
    fw = FOX_WIDTH
    fq, fk, fv = w_in[:, 0:fw], w_in[:, fw:2 * fw], w_in[:, 2 * fw:3 * fw]
    o = 3 * fw
    ff = w_in[:, o:o + FOX_HEADS]
    o += FOX_HEADS
    gqkv = w_in[:, o:o + 3 * GDN_WIDTH]
    o += 3 * GDN_WIDTH
    ga = w_in[:, o:o + GDN_HEADS]
    gb = w_in[:, o + GDN_HEADS:o + 2 * GDN_HEADS]
    gg = w_in[:, o + 2 * GDN_HEADS:]

    def pad_heads(w):
        w = w.reshape(d, FOX_HEADS, FOX_HEAD_DIM)
        return jnp.pad(w, ((0, 0), (0, 0), (0, LANES - FOX_HEAD_DIM))).reshape(d, FOX_PAD)

    small = jnp.pad(jnp.concatenate([ff, ga, gb], axis=1), ((0, 0), (0, LANES - FOX_HEADS - 2 * GDN_HEADS)))
    w = jnp.concatenate([pad_heads(fq), pad_heads(fk), fv, gqkv, gg, small], axis=1).astype(BF16)

    def lane_row(vals, start):
        return jnp.zeros((1, LANES), F32).at[0, start:start + vals.shape[0]].set(vals.astype(F32))

    qg = lane_row(fox_q_norm * fox_k_norm * (FOX_HEAD_DIM ** -0.5 * LOG2E), 0)
    brow = lane_row(fox_f_bias, LANE_CUM) + lane_row(gdn_dt_bias, LANE_GC)
    alog = lane_row(gdn_a_log, LANE_GC)

    pm = jnp.zeros((LANES, 2, FOX_HEADS, LANES), F32)
    hh = jnp.arange(FOX_HEADS)
    for piece in range(3):
        src = piece * PIECE_LANES + LANE_CUM + hh
        pm = pm.at[src, 0, hh, BIAS_LANE + piece].set(1.0)
        pm = pm.at[src, 1, hh, BIAS_LANE + 3 + piece].set(-1.0)
        pm = pm.at[3 * PIECE_LANES, 0, hh, BIAS_LANE + 3 + piece].set(1.0)
        pm = pm.at[3 * PIECE_LANES, 1, hh, BIAS_LANE + piece].set(1.0)
    pmat = pm.reshape(LANES, 2 * FOX_PAD).astype(BF16)
    return w, qg, brow, alog, pmat


def _proj_call(x3d, w_in, fox_q_norm, fox_k_norm, fox_f_bias, gdn_conv, gdn_a_log, gdn_dt_bias):
    b, s, d = x3d.shape
    tm = min(PROJ_TILE_M, s)
    w, qg, brow, alog, pmat = _proj_weights(w_in, fox_q_norm, fox_k_norm, fox_f_bias, gdn_a_log, gdn_dt_bias)
    r = jnp.arange(tm)
    upper = r[:, None] <= r[None, :]
    same_chunk = (r[:, None] // GDN_CHUNK) == (r[None, :] // GDN_CHUNK)
    triu = jnp.concatenate([upper, upper & same_chunk], axis=1).astype(BF16)
    brow_t = jnp.broadcast_to(brow[0, :GATE_ROWS, None], (GATE_ROWS, tm))
    alog_t = jnp.broadcast_to(alog[0, :GATE_ROWS, None], (GATE_ROWS, tm))

    n_sub = PROJ_SUBTILES if s % (PROJ_SUBTILES * tm) == 0 else 1
    ts = n_sub * tm

    def tok(width, dtype):
        return (pl.BlockSpec((1, ts, width), lambda bi, si: (bi, si, 0)),
                jax.ShapeDtypeStruct((b, s, width), dtype))

    outs = [(pl.BlockSpec((1, n_sub, FOX_PAD, tm), lambda bi, si: (bi, si, 0, 0)),
             jax.ShapeDtypeStruct((b, s // tm, FOX_PAD, tm), BF16)),
            tok(FOX_PAD, BF16),
            (pl.BlockSpec((1, n_sub, FOX_HEADS * VT_ROWS, tm), lambda bi, si: (bi, si, 0, 0)),
             jax.ShapeDtypeStruct((b, s // tm, FOX_HEADS * VT_ROWS, tm), BF16)),
            tok(GDN_WIDTH, BF16), tok(GDN_WIDTH, BF16), tok(GDN_WIDTH, BF16),
            tok(GDN_WIDTH, BF16), tok(LANES, F32),
            (pl.BlockSpec((1, GATE_ROWS, ts), lambda bi, si: (bi, 0, si)),
             jax.ShapeDtypeStruct((b, GATE_ROWS, s), F32))]
    consts = [w, gdn_conv.astype(F32), qg, brow_t, alog_t, triu, pmat]
    conv_scratch = pltpu.VMEM((tm + SUBLANES, 3 * GDN_WIDTH), F32)
    return pl.pallas_call(
        functools.partial(_proj_body, tm=tm, n_sub=n_sub),
        grid=(b, s // ts),
        in_specs=[pl.BlockSpec((1, ts, d), lambda bi, si: (bi, si, 0))] + [_const_spec(c.shape) for c in consts],
        out_specs=[o[0] for o in outs],
        out_shape=[o[1] for o in outs],
        scratch_shapes=[pltpu.VMEM((GATE_ROWS, LANES), F32), conv_scratch, conv_scratch,
                        pltpu.VMEM((n_sub, tm, LANES), BF16)],
        compiler_params=pltpu.CompilerParams(dimension_semantics=("arbitrary", "arbitrary"),
                                             vmem_limit_bytes=VMEM_LIMIT_BYTES),
        name="mixer_proj",
    )(x3d, *consts)


def _attn_body(q_ref, k_ref, vt_ref, o_ref, *, tile):
    i = pl.program_id(2)
    nh = ATTN_HEADS_PER_STEP
    dh = FOX_HEAD_DIM

    def logits(hh, j):
        off = pl.multiple_of(j * tile, tile)
        q_t = q_ref[0, 0, hh * LANES:(hh + 1) * LANES, :]
        k = k_ref[0, pl.ds(off, tile), hh * LANES:(hh + 1) * LANES]
        return _dot(k, q_t)

    def step(j, carry, diagonal):
        ahead = 2
        s_all = {hh: logits(hh, j) for hh in range(min(ahead, nh))}
        out = []
        for hh in range(nh):
            if hh + ahead < nh:
                s_all[hh + ahead] = logits(hh + ahead, j)
            m, acc = carry[hh]
            s = s_all.pop(hh)
            if diagonal:
                key = lax.broadcasted_iota(jnp.int32, s.shape, 0)
                qry = lax.broadcasted_iota(jnp.int32, s.shape, 1)
                s = jnp.where(qry >= key, s, -jnp.inf)
            m_new = jnp.maximum(m, jnp.max(s, axis=0, keepdims=True))
            p = jnp.exp2(s - m_new).astype(BF16)
            vt = vt_ref[0, j, hh * VT_ROWS:(hh + 1) * VT_ROWS, :]
            acc = jnp.exp2(m - m_new) * acc + _dot(vt, p)
            out.append((m_new, acc))
        return tuple(out)

    init = tuple((jnp.full((1, tile), -jnp.inf, F32), jnp.zeros((VT_ROWS, tile), F32)) for _ in range(nh))
    carry = lax.fori_loop(0, i, lambda j, c: step(j, c, False), init)
    carry = step(i, carry, True)
    out_t = jnp.concatenate([acc[:dh] / acc[dh:dh + 1] for _, acc in carry], axis=0)
    o_ref[0] = out_t.T.astype(BF16)


def _attn_call(qp, kp, vt):
    b, s, _ = kp.shape
    tile = min(ATTN_TILE, s)
    nh = ATTN_HEADS_PER_STEP
    return pl.pallas_call(
        functools.partial(_attn_body, tile=tile),
        grid=(b, FOX_HEADS // nh, s // tile),
        in_specs=[pl.BlockSpec((1, 1, nh * LANES, tile), lambda bi, hi, qi: (bi, qi, hi, 0)),
                  pl.BlockSpec((1, s, nh * LANES), lambda bi, hi, qi: (bi, 0, hi)),
                  pl.BlockSpec((1, s // tile, nh * VT_ROWS, tile), lambda bi, hi, qi: (bi, 0, hi, 0))],
        out_specs=pl.BlockSpec((1, tile, nh * FOX_HEAD_DIM), lambda bi, hi, qi: (bi, qi, hi)),
        out_shape=jax.ShapeDtypeStruct((b, s, FOX_WIDTH), BF16),
        compiler_params=pltpu.CompilerParams(dimension_semantics=("arbitrary", "arbitrary", "arbitrary"),
                                             vmem_limit_bytes=VMEM_LIMIT_BYTES),
        name="fox_attention",
    )(qp, kp, vt)


def _gdn_body(q_ref, k_ref, v_ref, sgg_ref, small_ref, smallt_ref, gain_ref, y_ref, state_ref, *, n_chunks):
    @pl.when(pl.program_id(1) == 0)
    def _():
        state_ref[...] = jnp.zeros_like(state_ref)

    n = GDN_CHUNK
    row = lax.broadcasted_iota(jnp.int32, (n, n), 0)
    col = lax.broadcasted_iota(jnp.int32, (n, n), 1)
    incl = row >= col
    strict = row > col
    eye = (row == col).astype(F32)

    pairs = [(c, h) for c in range(n_chunks) for h in range(GDN_HEADS)]
    pre = []
    for c, h in pairs:
        rows = slice(c * n, (c + 1) * n)
        hs = slice(h * GDN_HEAD_DIM, (h + 1) * GDN_HEAD_DIM)
        sm = small_ref[0, rows, :]
        smt = smallt_ref[0, :, rows]
        gcc = jnp.sum(jnp.where(col == LANE_GC + h, sm, 0.0), axis=1, keepdims=True)
        beta = jnp.sum(jnp.where(col == LANE_BETA + h, sm, 0.0), axis=1, keepdims=True)
        gcr = jnp.sum(jnp.where(row[:GATE_ROWS] == LANE_GC + h, smt, 0.0), axis=0, keepdims=True)
        g_last = jnp.sum(jnp.where(col[0:1, :] == n - 1, gcr, 0.0), axis=1, keepdims=True)
        k_b = k_ref[0, rows, hs]
        k = k_b.astype(F32)
        kb = k * beta
        k_tb = k_b.T
        pre.append(dict(rows=rows, hs=hs, gcc=gcc, g_last=g_last, kb=kb, k_tb=k_tb,
                        kd_t=(k_tb.astype(F32) * jnp.exp(g_last - gcr)).astype(BF16),
                        q_b=q_ref[0, rows, hs], vb=v_ref[0, rows, hs].astype(F32) * beta,
                        decay=jnp.exp(jnp.where(incl, gcc - gcr, -jnp.inf)), e_g=jnp.exp(gcc)))

    x_pow = [jnp.where(strict, -(_dot(p["kb"].astype(BF16), p["k_tb"]) * p["decay"]), 0.0) for p in pre]
    t = [eye + x for x in x_pow]
    x_pow = [_dot(x.astype(BF16), x.astype(BF16)) for x in x_pow]
    for step in range(1, NEUMANN_STEPS + 1):
        if step < NEUMANN_STEPS:
            x_b = [x.astype(BF16) for x in x_pow]
            both = [_dot(jnp.concatenate([xb, ti.astype(BF16)], axis=0), xb) for xb, ti in zip(x_b, t)]
            x_pow = [r[:n] for r in both]
            t = [ti + r[n:] for ti, r in zip(t, both)]
        else:
            t = [ti + _dot(ti.astype(BF16), x.astype(BF16)) for x, ti in zip(x_pow, t)]

    uw = [_dot(ti.astype(BF16), jnp.concatenate([p["vb"], p["kb"] * p["e_g"]], axis=1).astype(BF16))
          for ti, p in zip(t, pre)]
    attn = [jnp.where(incl, _dot(p["q_b"], p["k_tb"]) * p["decay"], 0.0).astype(BF16) for p in pre]

    states = [state_ref[h] for h in range(GDN_HEADS)]
    for c in range(n_chunks):
        idx = [c * GDN_HEADS + h for h in range(GDN_HEADS)]
        st_b = [st.astype(BF16) for st in states]
        v_new = [(uw[i][:, :GDN_HEAD_DIM] - _dot(uw[i][:, GDN_HEAD_DIM:].astype(BF16), st_b[h])).astype(BF16)
                 for h, i in enumerate(idx)]
        new_states = [states[h] * jnp.exp(pre[i]["g_last"]) + _dot(pre[i]["kd_t"], v_new[h])
                      for h, i in enumerate(idx)]
        for h, i in enumerate(idx):
            p = pre[i]
            qd = (p["q_b"].astype(F32) * p["e_g"]).astype(BF16)
            o = _dot(jnp.concatenate([qd, attn[i]], axis=1), jnp.concatenate([st_b[h], v_new[h]], axis=0))
            y = _rms_norm(o, gain_ref[...]) * sgg_ref[0, p["rows"], p["hs"]].astype(F32)
            y_ref[0, p["rows"], p["hs"]] = y.astype(BF16)
        states = new_states

    for h in range(GDN_HEADS):
        state_ref[h] = states[h]


def _gdn_call(gq, gk, gv, sgg, small, small_t, out_gain):
    b, s, _ = gq.shape
    n_chunks = min(GDN_CHUNKS_PER_STEP, s // GDN_CHUNK)
    ts = n_chunks * GDN_CHUNK
    tok = pl.BlockSpec((1, ts, GDN_WIDTH), lambda bi, ci: (bi, ci, 0))
    return pl.pallas_call(
        functools.partial(_gdn_body, n_chunks=n_chunks),
        grid=(b, s // ts),
        in_specs=[tok, tok, tok, tok,
                  pl.BlockSpec((1, ts, LANES), lambda bi, ci: (bi, ci, 0)),
                  pl.BlockSpec((1, GATE_ROWS, ts), lambda bi, ci: (bi, 0, ci)),
                  _const_spec((1, GDN_HEAD_DIM))],
        out_specs=tok,
        out_shape=jax.ShapeDtypeStruct((b, s, GDN_WIDTH), BF16),
        scratch_shapes=[pltpu.VMEM((GDN_HEADS, GDN_HEAD_DIM, GDN_HEAD_DIM), F32)],
        compiler_params=pltpu.CompilerParams(dimension_semantics=("arbitrary", "arbitrary"),
                                             vmem_limit_bytes=VMEM_LIMIT_BYTES),
        name="gated_delta_rule",
    )(gq, gk, gv, sgg, small, small_t, out_gain.reshape(1, GDN_HEAD_DIM).astype(F32))


def kernel(x, ffn1_norm, ffn1_w_in, ffn1_w_out, mix_norm, w_in, fox_q_norm, fox_k_norm, fox_f_bias,
           gdn_conv, gdn_a_log, gdn_dt_bias, gdn_out_norm, w_out, ffn2_norm, ffn2_w_in, ffn2_w_out):
    b, s, d = x.shape
    depth = ffn1_norm.shape[0]
    h = x.reshape(b * s, d)
    for l in range(depth):
        h, hn = _ffn_call(h, ffn1_norm[l], ffn1_w_in[l], ffn1_w_out[l], next_gain=mix_norm[l])
        qp, kp, vt, gq, gk, gv, sgg, small, small_t = _proj_call(
            hn.reshape(b, s, d), w_in[l], fox_q_norm[l], fox_k_norm[l], fox_f_bias[l],
            gdn_conv[l], gdn_a_log[l], gdn_dt_bias[l])
        y_fox = _attn_call(qp, kp, vt)
        y_gdn = _gdn_call(gq, gk, gv, sgg, small, small_t, gdn_out_norm[l])
        wo = w_out[l].astype(BF16)
        h = _ffn_call(h, ffn2_norm[l], ffn2_w_in[l], ffn2_w_out[l],
                      mix=(y_fox.reshape(b * s, FOX_WIDTH), y_gdn.reshape(b * s, GDN_WIDTH),
                           wo[:FOX_WIDTH], wo[FOX_WIDTH:]))
    return h.reshape(b, s, d)
```
